```python
import math
import jax, jax.numpy as jnp
from jax import lax
import numpy as np

D_MODEL = 1024
BATCH = 8
SEQ = 2048
DEPTH = 2

MOBA_HEADS = 8
HEAD_DIM = 64
MOBA_BLOCK = 256
MOBA_TOPK = 3
MOBA_Q_CHUNK = 16
MLA_HEADS = 8
MLA_Q_LORA = 384
MLA_KV_LORA = 256
MLA_NOPE = 64
MLA_ROPE = 32
MLA_V = 64
ROPE_THETA = 10000.0
DIFF_HEADS = 4
DIFF_QK = 64
DIFF_V = 2 * DIFF_QK
PLE_DIM = 256
Q_BLOCK = 128
NORM_EPS = 1e-5
NEG = -1e30

MOBA_W = MOBA_HEADS * HEAD_DIM
MLA_W = MLA_HEADS * MLA_V
DIFF_W = DIFF_HEADS * DIFF_V
DIFF_QK_W = DIFF_HEADS * 2 * DIFF_QK
IN_SIZES = (MOBA_W, MOBA_W, MOBA_W, MOBA_W,
            MLA_Q_LORA, MLA_KV_LORA, MLA_ROPE, MLA_W,
            DIFF_QK_W, DIFF_QK_W, DIFF_W, DIFF_W)
IN_WIDTH = sum(IN_SIZES)
N_BRANCH = 3
ALPHA = (2 * DEPTH) ** 0.25
BETA = (8 * DEPTH) ** -0.25

kernel_name = 'hybrid_moba_mla_diff_gated_deepnorm'


def _split_points():
    return [int(v) for v in np.cumsum(np.array(IN_SIZES))[:-1]]


def _alibi_slopes():
    n = MOBA_HEADS + DIFF_HEADS
    s = 2.0 ** (-8.0 * (np.arange(n) + 1) / n)
    diff_idx = np.arange(DIFF_HEADS) * (n // DIFF_HEADS)
    moba_idx = np.setdiff1d(np.arange(n), diff_idx)
    return (jnp.asarray(s[moba_idx], dtype=jnp.float32),
            jnp.asarray(s[diff_idx], dtype=jnp.float32))


def _rmsnorm(x, g, eps=1e-6):
    xf = x.astype(jnp.float32)
    y = xf * lax.rsqrt(jnp.mean(xf * xf, axis=-1, keepdims=True) + eps)
    return (y * g.astype(jnp.float32)).astype(x.dtype)


def _layernorm(x, g, b):
    xf = x.astype(jnp.float32)
    mu = jnp.mean(xf, axis=-1, keepdims=True)
    var = jnp.mean(jnp.square(xf - mu), axis=-1, keepdims=True)
    y = (xf - mu) * lax.rsqrt(var + NORM_EPS)
    return (y * g.astype(jnp.float32) + b.astype(jnp.float32)).astype(x.dtype)


def _rope(t, pos):
    d = t.shape[-1]
    freqs = ROPE_THETA ** (-jnp.arange(0, d, 2, dtype=jnp.float32) / d)
    ang = pos.astype(jnp.float32)[:, None] * freqs[None, :]
    cos, sin = jnp.cos(ang), jnp.sin(ang)
    tf = t.astype(jnp.float32)
    t1, t2 = tf[..., : d // 2], tf[..., d // 2:]
    return jnp.concatenate([t1 * cos - t2 * sin, t1 * sin + t2 * cos], axis=-1).astype(t.dtype)


def _split_heads(t, n):
    B, S, _ = t.shape
    return t.reshape(B, S, n, -1).transpose(0, 2, 1, 3)


def _merge_heads(t):
    B, H, S, d = t.shape
    return t.transpose(0, 2, 1, 3).reshape(B, S, H * d)


def moba_attention(q, k, v, slopes):
    B, H, S, dh = q.shape
    blk = MOBA_BLOCK
    nb = -(-S // blk)
    s_pad = nb * blk
    padw = ((0, 0), (0, 0), (0, s_pad - S), (0, 0))
    kp = jnp.pad(k, padw)
    vp = jnp.pad(v, padw)
    kb = kp.reshape(B, H, nb, blk, dh)
    vb = vp.reshape(B, H, nb, blk, dh)
    kmean = jnp.mean(kb.astype(jnp.float32), axis=3)
    pos = jnp.arange(S)
    gate = jnp.einsum('bhsd,bhnd->bhsn', q.astype(jnp.float32), kmean)
    past = jnp.arange(nb)[None, :] < (pos // blk)[:, None]
    gate = jnp.where(past, gate, -jnp.inf)
    kk = min(MOBA_TOPK, nb)
    _, sel = lax.top_k(gate, kk)

    C = MOBA_Q_CHUNK
    nc = S // C
    qc = q.reshape(B, H, nc, C, dh).transpose(2, 0, 1, 3, 4)
    selc = sel.reshape(B, H, nc, C, kk).transpose(2, 0, 1, 3, 4)
    bi = jnp.arange(B)[:, None, None, None]
    hi = jnp.arange(H)[None, :, None, None]
    scale = dh ** -0.5
    offs = jnp.arange(blk)

    def one(args):
        qi, si, c = args
        qpos = c * C + jnp.arange(C)
        ob = (c * C) // blk
        kg = kb[bi, hi, si]
        vg = vb[bi, hi, si]
        s_sel = jnp.einsum('bhqd,bhqjkd->bhqjk', qi, kg).astype(jnp.float32) * scale
        kpos_sel = si[..., None] * blk + offs
        dist_sel = (qpos[None, None, :, None, None] - kpos_sel).astype(jnp.float32)
        s_sel = s_sel - slopes[None, :, None, None, None] * dist_sel
        valid = jnp.arange(kk)[None, :] < (qpos // blk)[:, None]
        s_sel = jnp.where(valid[None, None, :, :, None], s_sel, NEG)
        ko = lax.dynamic_slice_in_dim(kp, ob * blk, blk, axis=2)
        vo = lax.dynamic_slice_in_dim(vp, ob * blk, blk, axis=2)
        dist_own = (qpos[:, None] - (ob * blk + offs)[None, :]).astype(jnp.float32)
        s_own = jnp.einsum('bhqd,bhkd->bhqk', qi, ko).astype(jnp.float32) * scale
        s_own = s_own - slopes[None, :, None, None] * dist_own
        s_own = jnp.where(dist_own >= 0, s_own, NEG)
        s_all = jnp.concatenate([s_sel.reshape(B, H, C, kk * blk), s_own], axis=-1)
        pr = jax.nn.softmax(s_all, axis=-1).astype(v.dtype)
        p_sel = pr[..., : kk * blk].reshape(B, H, C, kk, blk)
        p_own = pr[..., kk * blk:]
        return (jnp.einsum('bhqjk,bhqjkd->bhqd', p_sel, vg)
                + jnp.einsum('bhqk,bhkd->bhqd', p_own, vo))

    out = lax.map(one, (qc, selc, jnp.arange(nc)))
    return out.transpose(1, 2, 0, 3, 4).reshape(B, H, S, dh)


def mla_attention(q, k, v):
    B, H, S, dq = q.shape
    nq = S // Q_BLOCK
    scale = dq ** -0.5
    qb = q.reshape(B, H, nq, Q_BLOCK, dq).transpose(2, 0, 1, 3, 4)
    kpos = jnp.arange(S)

    def one(args):
        qi, i = args
        qpos = i * Q_BLOCK + jnp.arange(Q_BLOCK)
        s = jnp.einsum('bhqd,bhkd->bhqk', qi, k).astype(jnp.float32) * scale
        s = jnp.where(kpos[None, :] <= qpos[:, None], s, NEG)
        pr = jax.nn.softmax(s, axis=-1).astype(v.dtype)
        return jnp.einsum('bhqk,bhkd->bhqd', pr, v)

    out = lax.map(one, (qb, jnp.arange(nq)))
    return out.transpose(1, 2, 0, 3, 4).reshape(B, H, S, -1)


def diff_attention(q, k, v, slopes, lam):
    B, H, S = q.shape[:3]
    nq = S // Q_BLOCK
    scale = DIFF_QK ** -0.5
    qb = q.reshape(B, H, nq, Q_BLOCK, 2, DIFF_QK).transpose(2, 0, 1, 3, 4, 5)
    kpos = jnp.arange(S)

    def one(args):
        qi, i = args
        qpos = i * Q_BLOCK + jnp.arange(Q_BLOCK)
        dist = (qpos[:, None] - kpos[None, :]).astype(jnp.float32)
        s = jnp.einsum('bhqcd,bhkcd->bhcqk', qi, k).astype(jnp.float32) * scale
        s = s - slopes[None, :, None, None, None] * dist
        s = jnp.where(dist >= 0, s, NEG)
        pr = jax.nn.softmax(s, axis=-1)
        a = (pr[:, :, 0] - lam * pr[:, :, 1]).astype(v.dtype)
        return jnp.einsum('bhqk,bhkd->bhqd', a, v)

    out = lax.map(one, (qb, jnp.arange(nq)))
    return out.transpose(1, 2, 0, 3, 4).reshape(B, H, S, -1)


def _layer(x, p_i, li, w_in, gq, gkv, w_uq, w_ukv, lam_p, subln_g, w_a, w_b, w_c,
           w_m, b_m, w_o, ln_g, ln_b, w_pg, w_p, slopes_a, slopes_c):
    B, S, D = x.shape
    pos = jnp.arange(S)
    h = x @ w_in
    (a_q, a_k, a_v, a_z, b_cq, b_ckv, b_kr, b_z,
     c_q, c_k, c_v, c_z) = jnp.split(h, _split_points(), axis=-1)

    ya = moba_attention(_split_heads(a_q, MOBA_HEADS), _split_heads(a_k, MOBA_HEADS),
                        _split_heads(a_v, MOBA_HEADS), slopes_a)
    ya = _merge_heads(ya) * jax.nn.silu(a_z)

    cq = _rmsnorm(b_cq, gq)
    ckv = _rmsnorm(b_ckv, gkv)
    qh = _split_heads(cq @ w_uq, MLA_HEADS)
    q_b = jnp.concatenate([qh[..., :MLA_NOPE], _rope(qh[..., MLA_NOPE:], pos)], axis=-1)
    kvh = _split_heads(ckv @ w_ukv, MLA_HEADS)
    k_rope = _rope(b_kr[:, None], pos)
    k_b = jnp.concatenate([kvh[..., :MLA_NOPE],
                           jnp.broadcast_to(k_rope, (B, MLA_HEADS, S, MLA_ROPE))], axis=-1)
    yb = mla_attention(q_b, k_b, kvh[..., MLA_NOPE:])
    yb = _merge_heads(yb) * jax.nn.silu(b_z)

    lam_init = 0.8 - 0.6 * math.exp(-0.3 * li)
    lf = lam_p.astype(jnp.float32)
    lam = (jnp.exp(jnp.sum(lf[0] * lf[1])) - jnp.exp(jnp.sum(lf[2] * lf[3])) + lam_init)
    qc = c_q.reshape(B, S, DIFF_HEADS, 2, DIFF_QK).transpose(0, 2, 1, 3, 4)
    kc = c_k.reshape(B, S, DIFF_HEADS, 2, DIFF_QK).transpose(0, 2, 1, 3, 4)
    yc = diff_attention(qc, kc, _split_heads(c_v, DIFF_HEADS), slopes_c, lam)
    yc = _rmsnorm(yc, subln_g, eps=1e-5) * (1.0 - lam_init)
    yc = _merge_heads(yc) * jax.nn.silu(c_z)

    g = jax.nn.sigmoid(x @ w_m + b_m)
    ga, gb, gc = jnp.split(g, N_BRANCH, axis=-1)
    merged = ga * (ya @ w_a) + gb * (yb @ w_b) + gc * (yc @ w_c)
    out = merged @ w_o

    r = ALPHA * x + out
    r = r + jax.nn.sigmoid(r @ w_pg) * (p_i @ w_p)
    return _layernorm(r, ln_g, ln_b)


def setup_inputs(seed: int = 0) -> dict:
    key = jax.random.key(seed)
    ks = jax.random.split(key, 20)
    f32 = jnp.float32

    def nrm(k, shape, scale):
        return jax.random.normal(k, shape, f32) * scale

    L, D = DEPTH, D_MODEL
    return {
        'x': nrm(ks[0], (BATCH, SEQ, D), 1.0),
        'p': nrm(ks[1], (L, BATCH, SEQ, PLE_DIM), 1.0),
        'w_in': nrm(ks[2], (L, D, IN_WIDTH), D ** -0.5),
        'mla_q_norm_g': 1.0 + nrm(ks[3], (L, MLA_Q_LORA), 0.01),
        'mla_kv_norm_g': 1.0 + nrm(ks[4], (L, MLA_KV_LORA), 0.01),
        'mla_w_uq': nrm(ks[5], (L, MLA_Q_LORA, MLA_HEADS * (MLA_NOPE + MLA_ROPE)), MLA_Q_LORA ** -0.5),
        'mla_w_ukv': nrm(ks[6], (L, MLA_KV_LORA, MLA_HEADS * (MLA_NOPE + MLA_V)), MLA_KV_LORA ** -0.5),
        'diff_lambda': nrm(ks[7], (L, 4, DIFF_QK), 0.1),
        'diff_subln_g': 1.0 + nrm(ks[8], (L, DIFF_V), 0.01),
        'w_branch_a': nrm(ks[9], (L, MOBA_W, D), BETA * MOBA_W ** -0.5),
        'w_branch_b': nrm(ks[10], (L, MLA_W, D), BETA * MLA_W ** -0.5),
        'w_branch_c': nrm(ks[11], (L, DIFF_W, D), BETA * DIFF_W ** -0.5),
        'w_merge': nrm(ks[12], (L, D, N_BRANCH * D), D ** -0.5),
        'b_merge': nrm(ks[13], (L, N_BRANCH * D), 0.01),
        'w_out': nrm(ks[14], (L, D, D), BETA * D ** -0.5),
        'ln_g': 1.0 + nrm(ks[15], (L, D), 0.01),
        'ln_b': nrm(ks[16], (L, D), 0.01),
        'w_ple_gate': nrm(ks[17], (L, D, D), D ** -0.5),
        'w_ple': nrm(ks[18], (L, PLE_DIM, D), BETA * PLE_DIM ** -0.5),
    }


def reference(x, p, w_in, mla_q_norm_g, mla_kv_norm_g, mla_w_uq, mla_w_ukv, diff_lambda,
              diff_subln_g, w_branch_a, w_branch_b, w_branch_c, w_merge, b_merge, w_out,
              ln_g, ln_b, w_ple_gate, w_ple):
    slopes_a, slopes_c = _alibi_slopes()
    h = x
    for i in range(DEPTH):
        h = _layer(h, p[i], i, w_in[i], mla_q_norm_g[i], mla_kv_norm_g[i], mla_w_uq[i],
                   mla_w_ukv[i], diff_lambda[i], diff_subln_g[i], w_branch_a[i], w_branch_b[i],
                   w_branch_c[i], w_merge[i], b_merge[i], w_out[i], ln_g[i], ln_b[i],
                   w_ple_gate[i], w_ple[i], slopes_a, slopes_c)
    return h
```

```python
import functools
import math

import numpy as np
import jax
import jax.numpy as jnp
from jax import lax
from jax.experimental import pallas as pl
from jax.experimental.pallas import tpu as pltpu

D_MODEL = 1024
BATCH = 8
SEQ = 2048
DEPTH = 2
MOBA_HEADS = 8
HEAD_DIM = 64
MOBA_BLOCK = 256
MOBA_TOPK = 3
MLA_HEADS = 8
MLA_Q_LORA = 384
MLA_KV_LORA = 256
MLA_NOPE = 64
MLA_ROPE = 32
MLA_V = 64
ROPE_THETA = 10000.0
DIFF_HEADS = 4
DIFF_QK = 64
DIFF_V = 2 * DIFF_QK
PLE_DIM = 256
NORM_EPS = 1e-5
NEG = -1e30
N_BRANCH = 3
ALPHA = (2 * DEPTH) ** 0.25

MOBA_W = MOBA_HEADS * HEAD_DIM
MLA_W = MLA_HEADS * MLA_V
DIFF_W = DIFF_HEADS * DIFF_V
N_TOK = BATCH * SEQ
N_KV_BLOCKS = SEQ // MOBA_BLOCK

LANES = 128
VMEM_LIMIT = 54 * 1024 * 1024
TM = 256
TQ = 256
TK = 256

OFF_A = 0
OFF_CQ = OFF_A + 4 * MOBA_W
OFF_CKV = OFF_CQ + MLA_Q_LORA
OFF_KR = OFF_CKV + MLA_KV_LORA
OFF_KRS = OFF_KR + LANES
OFF_BZ = OFF_KRS + LANES
OFF_C = OFF_BZ + MLA_W
W_CAT = OFF_C + 4 * DIFF_W

F32 = jnp.float32
BF16 = jnp.bfloat16


def _dot(a, b):
    return jnp.dot(a, b, preferred_element_type=F32)


def _dot_nt(a, b):
    return lax.dot_general(a, b, (((1,), (1,)), ((), ())), preferred_element_type=F32)


def _alibi_slopes():
    n = MOBA_HEADS + DIFF_HEADS
    s = 2.0 ** (-8.0 * (np.arange(n) + 1) / n)
    diff_idx = np.arange(DIFF_HEADS) * (n // DIFF_HEADS)
    moba_idx = np.setdiff1d(np.arange(n), diff_idx)
    return (jnp.asarray(s[moba_idx], dtype=F32), jnp.asarray(s[diff_idx], dtype=F32))


def _rope_tables():
    d = MLA_ROPE
    freqs = ROPE_THETA ** (-np.arange(0, d, 2, dtype=np.float32) / d)
    ang = np.arange(SEQ, dtype=np.float32)[:, None] * freqs[None, :]
    cos, sin = np.cos(ang), np.sin(ang)
    cos_t = np.zeros((SEQ, LANES), np.float32)
    sin_t = np.zeros((SEQ, LANES), np.float32)
    cos_t[:, :MLA_NOPE] = 1.0
    cos_t[:, MLA_NOPE:MLA_NOPE + d] = np.concatenate([cos, cos], axis=-1)
    sin_t[:, MLA_NOPE:MLA_NOPE + d] = np.concatenate([sin, sin], axis=-1)
    return jnp.asarray(cos_t), jnp.asarray(sin_t)


def _rot_half_cols(w):
    half = MLA_ROPE // 2
    return jnp.concatenate([-w[..., half:], w[..., :half]], axis=-1)


def _proj_kernel(x_ref, w_ref, wq_ref, wqs_ref, wkn_ref, wv_ref, gq_ref, gkv_ref,
                 cos_ref, sin_ref,
                 aq_ref, ak_ref, av_ref, az_ref, kmean_ref,
                 bq_ref, bk_ref, bv_ref, bz_ref,
                 cq_ref, ck_ref, cv_ref, cz_ref):
    xb = x_ref[...].astype(BF16)

    def proj(off, width):
        return _dot(xb, w_ref[:, off:off + width])

    aq_ref[...] = (proj(OFF_A, MOBA_W) * (HEAD_DIM ** -0.5)).astype(BF16)
    ka = proj(OFF_A + MOBA_W, MOBA_W)
    ak_ref[...] = ka.astype(BF16)
    kmean_ref[0] = jnp.mean(ka, axis=0, keepdims=True)
    av_ref[...] = proj(OFF_A + 2 * MOBA_W, MOBA_W).astype(BF16)
    az_ref[...] = proj(OFF_A + 3 * MOBA_W, MOBA_W).astype(BF16)

    cos_t = cos_ref[...]
    sin_t = sin_ref[...]
    cq = proj(OFF_CQ, MLA_Q_LORA)
    cqn = cq * lax.rsqrt(jnp.mean(cq * cq, axis=-1, keepdims=True) + 1e-6) * gq_ref[...]
    cqn = cqn.astype(BF16)
    qa = _dot(cqn, wq_ref[...])
    qb = _dot(cqn, wqs_ref[...])
    qscale = (MLA_NOPE + MLA_ROPE) ** -0.5
    cos_q = cos_t * qscale
    sin_q = sin_t * qscale
    for h in range(MLA_HEADS):
        sl = slice(h * LANES, (h + 1) * LANES)
        bq_ref[:, sl] = (qa[:, sl] * cos_q + qb[:, sl] * sin_q).astype(BF16)

    ckv = proj(OFF_CKV, MLA_KV_LORA)
    ckvn = ckv * lax.rsqrt(jnp.mean(ckv * ckv, axis=-1, keepdims=True) + 1e-6) * gkv_ref[...]
    ckvn = ckvn.astype(BF16)
    kn = _dot(ckvn, wkn_ref[...])
    krot = proj(OFF_KR, LANES) * cos_t + proj(OFF_KRS, LANES) * sin_t
    for h in range(MLA_HEADS):
        sl = slice(h * LANES, (h + 1) * LANES)
        bk_ref[:, sl] = (kn[:, sl] + krot).astype(BF16)
    bv_ref[...] = _dot(ckvn, wv_ref[...]).astype(BF16)
    bz_ref[...] = proj(OFF_BZ, MLA_W).astype(BF16)

    cq_ref[...] = (proj(OFF_C, DIFF_W) * (DIFF_QK ** -0.5)).astype(BF16)
    ck_ref[...] = proj(OFF_C + DIFF_W, DIFF_W).astype(BF16)
    cv_ref[...] = proj(OFF_C + 2 * DIFF_W, DIFF_W).astype(BF16)
    cz_ref[...] = proj(OFF_C + 3 * DIFF_W, DIFF_W).astype(BF16)


def _const_spec(shape):
    nd = len(shape)
    return pl.BlockSpec(shape, lambda i: (0,) * nd)


def _proj_call(x2, wcat, wq, wqs, wkn, wv, gq, gkv, cos_t, sin_t):
    n_tiles = N_TOK // TM
    pos_tiles = SEQ // TM
    row = lambda w: pl.BlockSpec((TM, w), lambda i: (i, 0))
    tab = pl.BlockSpec((TM, LANES), lambda i: (i % pos_tiles, 0))
    out_shapes = []
    out_specs = []

    def add(width, dtype=BF16):
        out_shapes.append(jax.ShapeDtypeStruct((N_TOK, width), dtype))
        out_specs.append(row(width))

    add(MOBA_W); add(MOBA_W); add(MOBA_W); add(MOBA_W)
    out_shapes.append(jax.ShapeDtypeStruct((n_tiles, 1, MOBA_W), F32))
    out_specs.append(pl.BlockSpec((1, 1, MOBA_W), lambda i: (i, 0, 0)))
    add(MLA_HEADS * LANES); add(MLA_HEADS * LANES); add(MLA_W); add(MLA_W)
    add(DIFF_W); add(DIFF_W); add(DIFF_W); add(DIFF_W)
    return pl.pallas_call(
        _proj_kernel,
        grid=(n_tiles,),
        in_specs=[row(D_MODEL), _const_spec(wcat.shape), _const_spec(wq.shape),
                  _const_spec(wqs.shape), _const_spec(wkn.shape), _const_spec(wv.shape),
                  _const_spec(gq.shape), _const_spec(gkv.shape), tab, tab],
        out_specs=out_specs,
        out_shape=out_shapes,
        compiler_params=pltpu.CompilerParams(
            dimension_semantics=("arbitrary",), vmem_limit_bytes=VMEM_LIMIT),
        name="proj",
    )(x2, wcat, wq, wqs, wkn, wv, gq, gkv, cos_t, sin_t)


def _flash_head(qm, k_ref, v_ref, kcol, qi, alibi, blk_const):
    row_i = lax.broadcasted_iota(jnp.int32, (TQ, TK), 0)
    col_i = lax.broadcasted_iota(jnp.int32, (TQ, TK), 1)
    causal = col_i <= row_i

    def scores(j):
        start = pl.multiple_of(j * TK, TK)
        kj = k_ref[pl.ds(start, TK), kcol]
        s = _dot_nt(qm, kj)
        if alibi is not None:
            s = s + alibi
        return s, v_ref[pl.ds(start, TK), :]

    s, vj = scores(qi)
    s = jnp.where(causal, s, NEG)
    m = jnp.max(s, axis=-1, keepdims=True)
    p = jnp.exp(s - m)
    l = jnp.sum(p, axis=-1, keepdims=True)
    acc = _dot(p.astype(BF16), vj)

    def body(j, carry):
        m, l, acc = carry
        s, vj = scores(j)
        mb = jnp.max(s, axis=-1, keepdims=True)
        if blk_const is not None:
            cj = blk_const(j)
            mb = mb + cj
        m_new = jnp.maximum(m, mb)
        a = jnp.exp(m - m_new)
        shift = m_new - cj if blk_const is not None else m_new
        p = jnp.exp(s - shift)
        l = a * l + jnp.sum(p, axis=-1, keepdims=True)
        acc = a * acc + _dot(p.astype(BF16), vj)
        return m_new, l, acc

    m, l, acc = lax.fori_loop(0, qi, body, (m, l, acc))
    return acc, l


def _tile_rel():
    row_i = lax.broadcasted_iota(jnp.int32, (TQ, TK), 0)
    col_i = lax.broadcasted_iota(jnp.int32, (TQ, TK), 1)
    return (col_i - row_i).astype(F32)


def _moba_kernel(slopes_ref, q_ref, k_ref, v_ref, kmean_ref, o_ref):
    hp = pl.program_id(1)
    qi = pl.program_id(2)
    q2 = q_ref[...]
    lane_q = lax.broadcasted_iota(jnp.int32, (TQ, LANES), 1)
    lane_f = lane_q.astype(F32)
    rel = _tile_rel()
    km = jnp.concatenate(
        [kmean_ref[...], jnp.zeros((LANES - N_KV_BLOCKS, LANES), F32)], axis=0).astype(BF16)
    out = jnp.zeros((TQ, LANES), F32)
    for hh in range(2):
        slope = slopes_ref[hp * 2 + hh]
        in_head = (lane_q // HEAD_DIM) == hh
        qm = jnp.where(in_head, q2, jnp.zeros_like(q2))

        g = _dot_nt(qm, km)
        cur = jnp.where(lane_q < qi, g, -jnp.inf)
        selb = jnp.full((TQ, LANES), NEG, F32)
        for _ in range(MOBA_TOPK):
            mx = jnp.max(cur, axis=-1, keepdims=True)
            is_max = (cur == mx) & (cur > -jnp.inf)
            idx = jnp.min(jnp.where(is_max, lane_f, float(LANES)), axis=-1, keepdims=True)
            pick = lane_f == idx
            selb = jnp.where(pick, 0.0, selb)
            cur = jnp.where(pick, -jnp.inf, cur)

        def blk_const(j, selb=selb, slope=slope):
            sel_j = jnp.max(jnp.where(lane_q == j, selb, NEG), axis=-1, keepdims=True)
            return sel_j + slope * ((j - qi) * TK).astype(F32)

        acc, l = _flash_head(qm, k_ref, v_ref, slice(None), qi, slope * rel, blk_const)
        out = jnp.where(in_head, acc / l, out)
    o_ref[...] = out.astype(BF16)


def _moba_call(slopes, q, k, v, kmean):
    nq = SEQ // TQ
    return pl.pallas_call(
        _moba_kernel,
        grid=(BATCH, MOBA_HEADS // 2, nq),
        in_specs=[
            pl.BlockSpec(memory_space=pltpu.SMEM),
            pl.BlockSpec((TQ, LANES), lambda b, h, i: (b * nq + i, h)),
            pl.BlockSpec((SEQ, LANES), lambda b, h, i: (b, h)),
            pl.BlockSpec((SEQ, LANES), lambda b, h, i: (b, h)),
            pl.BlockSpec((None, N_KV_BLOCKS, LANES), lambda b, h, i: (b, 0, h)),
        ],
        out_specs=pl.BlockSpec((TQ, LANES), lambda b, h, i: (b * nq + i, h)),
        out_shape=jax.ShapeDtypeStruct((N_TOK, MOBA_W), BF16),
        compiler_params=pltpu.CompilerParams(
            dimension_semantics=("arbitrary", "arbitrary", "arbitrary"),
            vmem_limit_bytes=VMEM_LIMIT),
        name="moba",
    )(slopes, q, k, v, kmean)


def _mla_kernel(q_ref, k_ref, v_ref, o_ref):
    qi = pl.program_id(2)
    lane_q = lax.broadcasted_iota(jnp.int32, (TQ, LANES), 1)
    out = jnp.zeros((TQ, LANES), F32)
    for hh in range(2):
        sl = slice(hh * LANES, (hh + 1) * LANES)
        acc, l = _flash_head(q_ref[:, sl], k_ref, v_ref, sl, qi, None, None)
        out = jnp.where((lane_q // MLA_V) == hh, acc / l, out)
    o_ref[...] = out.astype(BF16)


def _mla_call(q, k, v):
    nq = SEQ // TQ
    return pl.pallas_call(
        _mla_kernel,
        grid=(BATCH, MLA_HEADS // 2, nq),
        in_specs=[
            pl.BlockSpec((TQ, 2 * LANES), lambda b, h, i: (b * nq + i, h)),
            pl.BlockSpec((SEQ, 2 * LANES), lambda b, h, i: (b, h)),
            pl.BlockSpec((SEQ, LANES), lambda b, h, i: (b, h)),
        ],
        out_specs=pl.BlockSpec((TQ, LANES), lambda b, h, i: (b * nq + i, h)),
        out_shape=jax.ShapeDtypeStruct((N_TOK, MLA_W), BF16),
        compiler_params=pltpu.CompilerParams(
            dimension_semantics=("arbitrary", "arbitrary", "arbitrary"),
            vmem_limit_bytes=VMEM_LIMIT),
        name="mla",
    )(q, k, v)


def _diff_kernel(lam_init, slopes_ref, q_ref, k_ref, v_ref, lam_ref, g_ref, o_ref):
    h = pl.program_id(1)
    qi = pl.program_id(2)
    slope = slopes_ref[h]
    q2 = q_ref[...]
    lane_q = lax.broadcasted_iota(jnp.int32, (TQ, LANES), 1)
    alibi = slope * _tile_rel()

    def blk_const(j):
        return jnp.full((TQ, 1), slope * ((j - qi) * TK).astype(F32), F32)

    outs = []
    for c in range(2):
        qm = jnp.where((lane_q // DIFF_QK) == c, q2, jnp.zeros_like(q2))
        acc, l = _flash_head(qm, k_ref, v_ref, slice(None), qi, alibi, blk_const)
        outs.append(acc / l)

    lf = lam_ref[...]
    e1 = jnp.exp(jnp.sum(lf[0:1, :] * lf[1:2, :], axis=-1, keepdims=True))
    e2 = jnp.exp(jnp.sum(lf[2:3, :] * lf[3:4, :], axis=-1, keepdims=True))
    lam = e1 - e2 + lam_init
    y = outs[0] - lam * outs[1]
    y = y * lax.rsqrt(jnp.mean(y * y, axis=-1, keepdims=True) + 1e-5) * g_ref[...]
    o_ref[...] = (y * (1.0 - lam_init)).astype(BF16)


def _diff_call(lam_init, slopes, q, k, v, lam_p, subln_g):
    nq = SEQ // TQ
    return pl.pallas_call(
        functools.partial(_diff_kernel, lam_init),
        grid=(BATCH, DIFF_HEADS, nq),
        in_specs=[
            pl.BlockSpec(memory_space=pltpu.SMEM),
            pl.BlockSpec((TQ, LANES), lambda b, h, i: (b * nq + i, h)),
            pl.BlockSpec((SEQ, LANES), lambda b, h, i: (b, h)),
            pl.BlockSpec((SEQ, LANES), lambda b, h, i: (b, h)),
            pl.BlockSpec(lam_p.shape, lambda b, h, i: (0, 0)),
            pl.BlockSpec(subln_g.shape, lambda b, h, i: (0, 0)),
        ],
        out_specs=pl.BlockSpec((TQ, LANES), lambda b, h, i: (b * nq + i, h)),
        out_shape=jax.ShapeDtypeStruct((N_TOK, DIFF_W), BF16),
        compiler_params=pltpu.CompilerParams(
            dimension_semantics=("arbitrary", "arbitrary", "arbitrary"),
            vmem_limit_bytes=VMEM_LIMIT),
        name="diff",
    )(slopes, q, k, v, lam_p, subln_g)


def _merge_kernel(x_ref, ya_ref, yb_ref, yc_ref, za_ref, zb_ref, zc_ref, p_ref,
                  wm_ref, bm_ref, wa_ref, wb_ref, wc_ref, wo_ref, wpg_ref, wp_ref,
                  lng_ref, lnb_ref, o_ref):
    x = x_ref[...]
    xb = x.astype(BF16)
    merged = jnp.zeros((TM, D_MODEL), F32)
    branches = ((ya_ref, za_ref, wa_ref), (yb_ref, zb_ref, wb_ref), (yc_ref, zc_ref, wc_ref))
    for i, (y_ref, z_ref, w_ref) in enumerate(branches):
        sl = slice(i * D_MODEL, (i + 1) * D_MODEL)
        gate = jax.nn.sigmoid(_dot(xb, wm_ref[:, sl]) + bm_ref[:, sl])
        z = z_ref[...].astype(F32)
        y = y_ref[...].astype(F32) * (z * jax.nn.sigmoid(z))
        merged = merged + gate * _dot(y.astype(BF16), w_ref[...])
    out = _dot(merged.astype(BF16), wo_ref[...])
    r = ALPHA * x + out
    ple = _dot(p_ref[...].astype(BF16), wp_ref[...])
    r = r + jax.nn.sigmoid(_dot(r.astype(BF16), wpg_ref[...])) * ple
    mu = jnp.mean(r, axis=-1, keepdims=True)
    d = r - mu
    var = jnp.mean(d * d, axis=-1, keepdims=True)
    o_ref[...] = d * lax.rsqrt(var + NORM_EPS) * lng_ref[...] + lnb_ref[...]


def _merge_call(x2, ya, yb, yc, za, zb, zc, p2, wm, bm, wa, wb, wc, wo, wpg, wp, lng, lnb):
    row = lambda w: pl.BlockSpec((TM, w), lambda i: (i, 0))
    consts = (wm, bm, wa, wb, wc, wo, wpg, wp, lng, lnb)
    return pl.pallas_call(
        _merge_kernel,
        grid=(N_TOK // TM,),
        in_specs=[row(D_MODEL)] + [row(MOBA_W)] * 6 + [row(PLE_DIM)]
                 + [_const_spec(c.shape) for c in consts],
        out_specs=row(D_MODEL),
        out_shape=jax.ShapeDtypeStruct((N_TOK, D_MODEL), F32),
        compiler_params=pltpu.CompilerParams(
            dimension_semantics=("arbitrary",), vmem_limit_bytes=VMEM_LIMIT),
        name="merge",
    )(x2, ya, yb, yc, za, zb, zc, p2, *consts)


def _prep_proj_weights(w_in, w_uq, w_ukv):
    pts = np.cumsum([MOBA_W] * 4 + [MLA_Q_LORA, MLA_KV_LORA, MLA_ROPE, MLA_W] + [DIFF_W] * 4)
    a, cq, ckv, kr, bz, c = (w_in[:, :pts[3]], w_in[:, pts[3]:pts[4]], w_in[:, pts[4]:pts[5]],
                             w_in[:, pts[5]:pts[6]], w_in[:, pts[6]:pts[7]], w_in[:, pts[7]:])
    z_lo = jnp.zeros((D_MODEL, MLA_NOPE), F32)
    z_hi = jnp.zeros((D_MODEL, LANES - MLA_NOPE - MLA_ROPE), F32)
    kr128 = jnp.concatenate([z_lo, kr, z_hi], axis=-1)
    krs128 = jnp.concatenate([z_lo, _rot_half_cols(kr), z_hi], axis=-1)
    wcat = jnp.concatenate([a, cq, ckv, kr128, krs128, bz, c], axis=-1).astype(BF16)

    dq = MLA_NOPE + MLA_ROPE
    uq = w_uq.reshape(MLA_Q_LORA, MLA_HEADS, dq)
    pad_q = jnp.zeros((MLA_Q_LORA, MLA_HEADS, LANES - dq), F32)
    wq = jnp.concatenate([uq, pad_q], axis=-1)
    wqs = jnp.concatenate([jnp.zeros_like(uq[..., :MLA_NOPE]),
                           _rot_half_cols(uq[..., MLA_NOPE:]), pad_q], axis=-1)
    ukv = w_ukv.reshape(MLA_KV_LORA, MLA_HEADS, MLA_NOPE + MLA_V)
    wkn = jnp.concatenate([ukv[..., :MLA_NOPE],
                           jnp.zeros((MLA_KV_LORA, MLA_HEADS, LANES - MLA_NOPE), F32)], axis=-1)
    wv = ukv[..., MLA_NOPE:]
    flat = lambda w: w.reshape(w.shape[0], -1).astype(BF16)
    return wcat, flat(wq), flat(wqs), flat(wkn), flat(wv)


def kernel(x, p, w_in, mla_q_norm_g, mla_kv_norm_g, mla_w_uq, mla_w_ukv, diff_lambda, diff_subln_g, w_branch_a, w_branch_b, w_branch_c, w_merge, b_merge, w_out, ln_g, ln_b, w_ple_gate, w_ple):
    slopes_a, slopes_c = _alibi_slopes()
    cos_t, sin_t = _rope_tables()
    h = x.reshape(N_TOK, D_MODEL)
    for i in range(DEPTH):
        wcat, wq, wqs, wkn, wv = _prep_proj_weights(w_in[i], mla_w_uq[i], mla_w_ukv[i])
        (aq, ak, av, az, kmean, bq, bk, bv, bz, cq, ck, cv, cz) = _proj_call(
            h, wcat, wq, wqs, wkn, wv, mla_q_norm_g[i][None, :], mla_kv_norm_g[i][None, :],
            cos_t, sin_t)
        kmean = kmean.reshape(BATCH, N_KV_BLOCKS, MOBA_W)
        ya = _moba_call(slopes_a, aq, ak, av, kmean)
        yb = _mla_call(bq, bk, bv)
        lam_init = 0.8 - 0.6 * math.exp(-0.3 * i)
        yc = _diff_call(lam_init, slopes_c, cq, ck, cv, diff_lambda[i], diff_subln_g[i][None, :])
        bf = lambda w: w.astype(BF16)
        h = _merge_call(h, ya, yb, yc, az, bz, cz, p[i].reshape(N_TOK, PLE_DIM),
                        bf(w_merge[i]), b_merge[i][None, :], bf(w_branch_a[i]),
                        bf(w_branch_b[i]), bf(w_branch_c[i]), bf(w_out[i]), bf(w_ple_gate[i]),
                        bf(w_ple[i]), ln_g[i][None, :], ln_b[i][None, :])
    return h.reshape(BATCH, SEQ, D_MODEL)
```

```python
import functools
import math

import numpy as np
import jax
import jax.numpy as jnp
from jax import lax
from jax.experimental import pallas as pl
from jax.experimental.pallas import tpu as pltpu

D_MODEL = 1024
BATCH = 8
SEQ = 2048
DEPTH = 2
MOBA_HEADS = 8
HEAD_DIM = 64
MOBA_BLOCK = 256
MOBA_TOPK = 3
MLA_HEADS = 8
MLA_Q_LORA = 384
MLA_KV_LORA = 256
MLA_NOPE = 64
MLA_ROPE = 32
MLA_V = 64
ROPE_THETA = 10000.0
DIFF_HEADS = 4
DIFF_QK = 64
DIFF_V = 2 * DIFF_QK
PLE_DIM = 256
NORM_EPS = 1e-5
NEG = -1e30
N_BRANCH = 3
ALPHA = (2 * DEPTH) ** 0.25
LOG2E = math.log2(math.e)

MOBA_W = MOBA_HEADS * HEAD_DIM
MLA_W = MLA_HEADS * MLA_V
DIFF_W = DIFF_HEADS * DIFF_V
N_TOK = BATCH * SEQ
N_KV_BLOCKS = SEQ // MOBA_BLOCK

LANES = 128
VMEM_LIMIT = 54 * 1024 * 1024
TM = 256
TQ = 256
TK = 256

OFF_A = 0
OFF_CQ = OFF_A + 4 * MOBA_W
OFF_CKV = OFF_CQ + MLA_Q_LORA
OFF_KR = OFF_CKV + MLA_KV_LORA
OFF_KRS = OFF_KR + LANES
OFF_BZ = OFF_KRS + LANES
OFF_C = OFF_BZ + MLA_W
W_CAT = OFF_C + 4 * DIFF_W

F32 = jnp.float32
BF16 = jnp.bfloat16


def _dot(a, b):
    return jnp.dot(a, b, preferred_element_type=F32)


def _dot_nt(a, b):
    return lax.dot_general(a, b, (((1,), (1,)), ((), ())), preferred_element_type=F32)


def _alibi_slopes():
    n = MOBA_HEADS + DIFF_HEADS
    s = 2.0 ** (-8.0 * (np.arange(n) + 1) / n)
    diff_idx = np.arange(DIFF_HEADS) * (n // DIFF_HEADS)
    moba_idx = np.setdiff1d(np.arange(n), diff_idx)
    return (jnp.asarray(s[moba_idx], dtype=F32), jnp.asarray(s[diff_idx], dtype=F32))


def _rope_tables():
    d = MLA_ROPE
    freqs = ROPE_THETA ** (-np.arange(0, d, 2, dtype=np.float32) / d)
    ang = np.arange(SEQ, dtype=np.float32)[:, None] * freqs[None, :]
    cos, sin = np.cos(ang), np.sin(ang)
    cos_t = np.zeros((SEQ, LANES), np.float32)
    sin_t = np.zeros((SEQ, LANES), np.float32)
    cos_t[:, :MLA_NOPE] = 1.0
    cos_t[:, MLA_NOPE:MLA_NOPE + d] = np.concatenate([cos, cos], axis=-1)
    sin_t[:, MLA_NOPE:MLA_NOPE + d] = np.concatenate([sin, sin], axis=-1)
    return jnp.asarray(cos_t), jnp.asarray(sin_t)


def _rot_half_cols(w):
    half = MLA_ROPE // 2
    return jnp.concatenate([-w[..., half:], w[..., :half]], axis=-1)


def _proj_kernel(x_ref, w_ref, wq_ref, wqs_ref, wkn_ref, wv_ref, gq_ref, gkv_ref,
                 cos_ref, sin_ref,
                 aq_ref, ak_ref, av_ref, az_ref, kmean_ref,
                 bq_ref, bk_ref, bv_ref, bz_ref,
                 cq_ref, ck_ref, cv_ref, cz_ref):
    xb = x_ref[...].astype(BF16)

    def proj(off, width):
        return _dot(xb, w_ref[:, off:off + width])

    aq_ref[...] = (proj(OFF_A, MOBA_W) * (HEAD_DIM ** -0.5 * LOG2E)).astype(BF16)
    ka = proj(OFF_A + MOBA_W, MOBA_W)
    ak_ref[...] = ka.astype(BF16)
    kmean_ref[0] = jnp.mean(ka, axis=0, keepdims=True)
    av_ref[...] = proj(OFF_A + 2 * MOBA_W, MOBA_W).astype(BF16)
    az_ref[...] = proj(OFF_A + 3 * MOBA_W, MOBA_W).astype(BF16)

    cos_t = cos_ref[...]
    sin_t = sin_ref[...]
    cq = proj(OFF_CQ, MLA_Q_LORA)
    cqn = cq * lax.rsqrt(jnp.mean(cq * cq, axis=-1, keepdims=True) + 1e-6) * gq_ref[...]
    cqn = cqn.astype(BF16)
    qa = _dot(cqn, wq_ref[...])
    qb = _dot(cqn, wqs_ref[...])
    qscale = (MLA_NOPE + MLA_ROPE) ** -0.5 * LOG2E
    cos_q = cos_t * qscale
    sin_q = sin_t * qscale
    for h in range(MLA_HEADS):
        sl = slice(h * LANES, (h + 1) * LANES)
        bq_ref[:, sl] = (qa[:, sl] * cos_q + qb[:, sl] * sin_q).astype(BF16)

    ckv = proj(OFF_CKV, MLA_KV_LORA)
    ckvn = ckv * lax.rsqrt(jnp.mean(ckv * ckv, axis=-1, keepdims=True) + 1e-6) * gkv_ref[...]
    ckvn = ckvn.astype(BF16)
    kn = _dot(ckvn, wkn_ref[...])
    krot = proj(OFF_KR, LANES) * cos_t + proj(OFF_KRS, LANES) * sin_t
    for h in range(MLA_HEADS):
        sl = slice(h * LANES, (h + 1) * LANES)
        bk_ref[:, sl] = (kn[:, sl] + krot).astype(BF16)
    bv_ref[...] = _dot(ckvn, wv_ref[...]).astype(BF16)
    bz_ref[...] = proj(OFF_BZ, MLA_W).astype(BF16)

    cq_ref[...] = (proj(OFF_C, DIFF_W) * (DIFF_QK ** -0.5 * LOG2E)).astype(BF16)
    ck_ref[...] = proj(OFF_C + DIFF_W, DIFF_W).astype(BF16)
    cv_ref[...] = proj(OFF_C + 2 * DIFF_W, DIFF_W).astype(BF16)
    cz_ref[...] = proj(OFF_C + 3 * DIFF_W, DIFF_W).astype(BF16)


def _const_spec(shape):
    nd = len(shape)
    return pl.BlockSpec(shape, lambda i: (0,) * nd)


def _proj_call(x2, wcat, wq, wqs, wkn, wv, gq, gkv, cos_t, sin_t):
    n_tiles = N_TOK // TM
    pos_tiles = SEQ // TM
    row = lambda w: pl.BlockSpec((TM, w), lambda i: (i, 0))
    tab = pl.BlockSpec((TM, LANES), lambda i: (i % pos_tiles, 0))
    out_shapes = []
    out_specs = []

    def add(width, dtype=BF16):
        out_shapes.append(jax.ShapeDtypeStruct((N_TOK, width), dtype))
        out_specs.append(row(width))

    add(MOBA_W); add(MOBA_W); add(MOBA_W); add(MOBA_W)
    out_shapes.append(jax.ShapeDtypeStruct((n_tiles, 1, MOBA_W), F32))
    out_specs.append(pl.BlockSpec((1, 1, MOBA_W), lambda i: (i, 0, 0)))
    add(MLA_HEADS * LANES); add(MLA_HEADS * LANES); add(MLA_W); add(MLA_W)
    add(DIFF_W); add(DIFF_W); add(DIFF_W); add(DIFF_W)
    return pl.pallas_call(
        _proj_kernel,
        grid=(n_tiles,),
        in_specs=[row(D_MODEL), _const_spec(wcat.shape), _const_spec(wq.shape),
                  _const_spec(wqs.shape), _const_spec(wkn.shape), _const_spec(wv.shape),
                  _const_spec(gq.shape), _const_spec(gkv.shape), tab, tab],
        out_specs=out_specs,
        out_shape=out_shapes,
        compiler_params=pltpu.CompilerParams(
            dimension_semantics=("arbitrary",), vmem_limit_bytes=VMEM_LIMIT),
        name="proj",
    )(x2, wcat, wq, wqs, wkn, wv, gq, gkv, cos_t, sin_t)


AUG_SEL = 0
AUG_POS = 8


def _split3(v):
    hi = v.astype(BF16).astype(F32)
    mid = (v - hi).astype(BF16).astype(F32)
    lo = (v - hi - mid).astype(BF16).astype(F32)
    return hi, mid, lo


def _key_aug_lanes(slope2, with_blocks):
    lane = lax.broadcasted_iota(jnp.int32, (SEQ, LANES), 1)
    kpos = lax.broadcasted_iota(jnp.int32, (SEQ, LANES), 0)
    hi, mid, lo = _split3(slope2 * kpos.astype(F32))
    x = jnp.where(lane == AUG_POS, hi, jnp.where(lane == AUG_POS + 1, mid,
                                                 jnp.where(lane == AUG_POS + 2, lo, 0.0)))
    if with_blocks:
        x = jnp.where(lane - AUG_SEL == kpos // MOBA_BLOCK, 1.0, x)
    return x.astype(BF16)


def _flash_chains(q_list, k_tile, v_tile, qi):
    n = len(q_list)
    row_i = lax.broadcasted_iota(jnp.int32, (TQ, TK), 0)
    col_i = lax.broadcasted_iota(jnp.int32, (TQ, TK), 1)
    causal = col_i <= row_i

    d0 = pl.multiple_of(qi * TK, TK)
    s = [jnp.where(causal, _dot_nt(q_list[c], k_tile(c, d0)), NEG) for c in range(n)]
    m = [jnp.max(s[c], axis=-1, keepdims=True) for c in range(n)]
    p = [jnp.exp2(s[c] - m[c]).astype(BF16) for c in range(n)]
    acc = [_dot(p[c], v_tile(c, d0)) for c in range(n)]

    def body(j, carry):
        m, acc = carry
        start = pl.multiple_of(j * TK, TK)
        s = [_dot_nt(q_list[c], k_tile(c, start)) for c in range(n)]
        m_new = [jnp.maximum(m[c], jnp.max(s[c], axis=-1, keepdims=True)) for c in range(n)]
        a = [jnp.exp2(m[c] - m_new[c]) for c in range(n)]
        p = [jnp.exp2(s[c] - m_new[c]).astype(BF16) for c in range(n)]
        acc = [a[c] * acc[c] + _dot(p[c], v_tile(c, start)) for c in range(n)]
        return tuple(m_new), tuple(acc)

    m, acc = lax.fori_loop(0, qi, body, (tuple(m), tuple(acc)))
    return acc


def _moba_kernel(slopes_ref, q_ref, k_ref, v_ref, kmean_ref, o_ref, kaug_ref, vaug_ref):
    hp = pl.program_id(1)
    qi = pl.program_id(2)
    lane_q = lax.broadcasted_iota(jnp.int32, (TQ, LANES), 1)
    lane_f = lane_q.astype(F32)

    @pl.when(qi == 0)
    def _():
        lane_k = lax.broadcasted_iota(jnp.int32, (SEQ, LANES), 1)
        k2 = k_ref[...]
        v2 = v_ref[...].astype(F32)
        for hh in range(2):
            kaug_ref[hh, :, :LANES] = k2
            kaug_ref[hh, :, LANES:] = _key_aug_lanes(slopes_ref[hp * 2 + hh] * LOG2E, True)
            own = (lane_k // HEAD_DIM) == hh
            ones_col = jnp.where(lane_k == (1 - hh) * HEAD_DIM, 1.0, 0.0)
            vaug_ref[hh] = jnp.where(own, v2, ones_col).astype(BF16)

    q2 = q_ref[...]
    km = jnp.concatenate(
        [kmean_ref[...], jnp.zeros((LANES - N_KV_BLOCKS, LANES), F32)], axis=0).astype(BF16)
    q_list = []
    for hh in range(2):
        in_head = (lane_q // HEAD_DIM) == hh
        qm = jnp.where(in_head, q2, jnp.zeros_like(q2))
        g = _dot_nt(qm, km)
        cur = jnp.where(lane_q < qi, g, -jnp.inf)
        selb = jnp.where(lane_q == qi, 0.0, NEG)
        for _ in range(MOBA_TOPK):
            mx = jnp.max(cur, axis=-1, keepdims=True)
            is_max = (cur == mx) & (cur > -jnp.inf)
            idx = jnp.min(jnp.where(is_max, lane_f, float(LANES)), axis=-1, keepdims=True)
            pick = lane_f == idx
            selb = jnp.where(pick, 0.0, selb)
            cur = jnp.where(pick, -jnp.inf, cur)
        qx = jnp.where(lane_q < N_KV_BLOCKS, selb,
                       jnp.where((lane_q >= AUG_POS) & (lane_q < AUG_POS + 3), 1.0, 0.0))
        q_list.append(jnp.concatenate([qm, qx.astype(BF16)], axis=1))

    acc = _flash_chains(
        q_list,
        lambda c, st: kaug_ref[c, pl.ds(st, TK), :],
        lambda c, st: vaug_ref[c, pl.ds(st, TK), :],
        qi)
    l0 = acc[0][:, HEAD_DIM:HEAD_DIM + 1]
    l1 = acc[1][:, 0:1]
    out = jnp.where(lane_q < HEAD_DIM, acc[0] / l0, acc[1] / l1)
    o_ref[...] = out.astype(BF16)


def _moba_call(slopes, q, k, v, kmean):
    nq = SEQ // TQ
    return pl.pallas_call(
        _moba_kernel,
        grid=(BATCH, MOBA_HEADS // 2, nq),
        in_specs=[
            pl.BlockSpec(memory_space=pltpu.SMEM),
            pl.BlockSpec((TQ, LANES), lambda b, h, i: (b * nq + i, h)),
            pl.BlockSpec((SEQ, LANES), lambda b, h, i: (b, h)),
            pl.BlockSpec((SEQ, LANES), lambda b, h, i: (b, h)),
            pl.BlockSpec((None, N_KV_BLOCKS, LANES), lambda b, h, i: (b, 0, h)),
        ],
        out_specs=pl.BlockSpec((TQ, LANES), lambda b, h, i: (b * nq + i, h)),
        out_shape=jax.ShapeDtypeStruct((N_TOK, MOBA_W), BF16),
        scratch_shapes=[pltpu.VMEM((2, SEQ, 2 * LANES), BF16),
                        pltpu.VMEM((2, SEQ, LANES), BF16)],
        compiler_params=pltpu.CompilerParams(
            dimension_semantics=("arbitrary", "arbitrary", "arbitrary"),
            vmem_limit_bytes=VMEM_LIMIT),
        name="moba",
    )(slopes, q, k, v, kmean)


MLA_GROUP = 4


def _mla_kernel(q_ref, k_ref, v_ref, o_ref, vaug_ref):
    qi = pl.program_id(2)
    lane_q = lax.broadcasted_iota(jnp.int32, (TQ, LANES), 1)

    @pl.when(qi == 0)
    def _():
        lane_k = lax.broadcasted_iota(jnp.int32, (SEQ, LANES), 1)
        for c in range(MLA_GROUP):
            v2 = v_ref[:, (c // 2) * LANES:(c // 2 + 1) * LANES].astype(F32)
            hh = c % 2
            own = (lane_k // MLA_V) == hh
            ones_col = jnp.where(lane_k == (1 - hh) * MLA_V, 1.0, 0.0)
            vaug_ref[c] = jnp.where(own, v2, ones_col).astype(BF16)

    q_list = [q_ref[:, c * LANES:(c + 1) * LANES] for c in range(MLA_GROUP)]
    acc = _flash_chains(
        q_list,
        lambda c, st: k_ref[pl.ds(st, TK), c * LANES:(c + 1) * LANES],
        lambda c, st: vaug_ref[c, pl.ds(st, TK), :],
        qi)
    for pr in range(MLA_GROUP // 2):
        a0, a1 = acc[2 * pr], acc[2 * pr + 1]
        out = jnp.where(lane_q < MLA_V, a0 / a0[:, MLA_V:MLA_V + 1], a1 / a1[:, 0:1])
        o_ref[:, pr * LANES:(pr + 1) * LANES] = out.astype(BF16)


def _mla_call(q, k, v):
    nq = SEQ // TQ
    g = MLA_GROUP
    return pl.pallas_call(
        _mla_kernel,
        grid=(BATCH, MLA_HEADS // g, nq),
        in_specs=[
            pl.BlockSpec((TQ, g * LANES), lambda b, h, i: (b * nq + i, h)),
            pl.BlockSpec((SEQ, g * LANES), lambda b, h, i: (b, h)),
            pl.BlockSpec((SEQ, g * MLA_V), lambda b, h, i: (b, h)),
        ],
        out_specs=pl.BlockSpec((TQ, g * MLA_V), lambda b, h, i: (b * nq + i, h)),
        out_shape=jax.ShapeDtypeStruct((N_TOK, MLA_W), BF16),
        scratch_shapes=[pltpu.VMEM((g, SEQ, LANES), BF16)],
        compiler_params=pltpu.CompilerParams(
            dimension_semantics=("arbitrary", "arbitrary", "arbitrary"),
            vmem_limit_bytes=VMEM_LIMIT),
        name="mla",
    )(q, k, v)


DIFF_GROUP = 2


def _diff_kernel(lam_init, slopes_ref, q_ref, k_ref, v_ref, lam_ref, g_ref, o_ref,
                 kaug_ref, vaug_ref):
    hg = pl.program_id(1)
    qi = pl.program_id(2)
    lane_q = lax.broadcasted_iota(jnp.int32, (TQ, LANES), 1)

    @pl.when(qi == 0)
    def _():
        lane_k = lax.broadcasted_iota(jnp.int32, (SEQ, LANES), 1)
        ones_col = jnp.where(lane_k == 0, 1.0, 0.0).astype(BF16)
        for hh in range(DIFF_GROUP):
            sl = slice(hh * LANES, (hh + 1) * LANES)
            kaug_ref[hh, :, :LANES] = k_ref[:, sl]
            kaug_ref[hh, :, LANES:] = _key_aug_lanes(
                slopes_ref[hg * DIFF_GROUP + hh] * LOG2E, False)
            vaug_ref[hh, :, :LANES] = v_ref[:, sl]
            vaug_ref[hh, :, LANES:] = ones_col

    qx = jnp.where((lane_q >= AUG_POS) & (lane_q < AUG_POS + 3), 1.0, 0.0).astype(BF16)
    q_list = []
    for hh in range(DIFF_GROUP):
        q2 = q_ref[:, hh * LANES:(hh + 1) * LANES]
        for c in range(2):
            qm = jnp.where((lane_q // DIFF_QK) == c, q2, jnp.zeros_like(q2))
            q_list.append(jnp.concatenate([qm, qx], axis=1))

    acc = _flash_chains(
        q_list,
        lambda c, st: kaug_ref[c // 2, pl.ds(st, TK), :],
        lambda c, st: vaug_ref[c // 2, pl.ds(st, TK), :],
        qi)

    lf = lam_ref[...]
    e1 = jnp.exp(jnp.sum(lf[0:1, :] * lf[1:2, :], axis=-1, keepdims=True))
    e2 = jnp.exp(jnp.sum(lf[2:3, :] * lf[3:4, :], axis=-1, keepdims=True))
    lam = e1 - e2 + lam_init
    for hh in range(DIFF_GROUP):
        a0, a1 = acc[2 * hh], acc[2 * hh + 1]
        o0 = a0[:, :LANES] / a0[:, LANES:LANES + 1]
        o1 = a1[:, :LANES] / a1[:, LANES:LANES + 1]
        y = o0 - lam * o1
        y = y * lax.rsqrt(jnp.mean(y * y, axis=-1, keepdims=True) + 1e-5) * g_ref[...]
        o_ref[:, hh * LANES:(hh + 1) * LANES] = (y * (1.0 - lam_init)).astype(BF16)


def _diff_call(lam_init, slopes, q, k, v, lam_p, subln_g):
    nq = SEQ // TQ
    g = DIFF_GROUP
    return pl.pallas_call(
        functools.partial(_diff_kernel, lam_init),
        grid=(BATCH, DIFF_HEADS // g, nq),
        in_specs=[
            pl.BlockSpec(memory_space=pltpu.SMEM),
            pl.BlockSpec((TQ, g * LANES), lambda b, h, i: (b * nq + i, h)),
            pl.BlockSpec((SEQ, g * LANES), lambda b, h, i: (b, h)),
            pl.BlockSpec((SEQ, g * LANES), lambda b, h, i: (b, h)),
            pl.BlockSpec(lam_p.shape, lambda b, h, i: (0, 0)),
            pl.BlockSpec(subln_g.shape, lambda b, h, i: (0, 0)),
        ],
        out_specs=pl.BlockSpec((TQ, g * LANES), lambda b, h, i: (b * nq + i, h)),
        out_shape=jax.ShapeDtypeStruct((N_TOK, DIFF_W), BF16),
        scratch_shapes=[pltpu.VMEM((g, SEQ, 2 * LANES), BF16),
                        pltpu.VMEM((g, SEQ, 2 * LANES), BF16)],
        compiler_params=pltpu.CompilerParams(
            dimension_semantics=("arbitrary", "arbitrary", "arbitrary"),
            vmem_limit_bytes=VMEM_LIMIT),
        name="diff",
    )(slopes, q, k, v, lam_p, subln_g)


def _merge_kernel(x_ref, ya_ref, yb_ref, yc_ref, za_ref, zb_ref, zc_ref, p_ref,
                  wm_ref, bm_ref, wa_ref, wb_ref, wc_ref, wo_ref, wpg_ref, wp_ref,
                  lng_ref, lnb_ref, o_ref):
    x = x_ref[...]
    xb = x.astype(BF16)
    merged = jnp.zeros((TM, D_MODEL), F32)
    branches = ((ya_ref, za_ref, wa_ref), (yb_ref, zb_ref, wb_ref), (yc_ref, zc_ref, wc_ref))
    for i, (y_ref, z_ref, w_ref) in enumerate(branches):
        sl = slice(i * D_MODEL, (i + 1) * D_MODEL)
        gate = jax.nn.sigmoid(_dot(xb, wm_ref[:, sl]) + bm_ref[:, sl])
        z = z_ref[...].astype(F32)
        y = y_ref[...].astype(F32) * (z * jax.nn.sigmoid(z))
        merged = merged + gate * _dot(y.astype(BF16), w_ref[...])
    out = _dot(merged.astype(BF16), wo_ref[...])
    r = ALPHA * x + out
    ple = _dot(p_ref[...].astype(BF16), wp_ref[...])
    r = r + jax.nn.sigmoid(_dot(r.astype(BF16), wpg_ref[...])) * ple
    mu = jnp.mean(r, axis=-1, keepdims=True)
    d = r - mu
    var = jnp.mean(d * d, axis=-1, keepdims=True)
    o_ref[...] = d * lax.rsqrt(var + NORM_EPS) * lng_ref[...] + lnb_ref[...]


def _merge_call(x2, ya, yb, yc, za, zb, zc, p2, wm, bm, wa, wb, wc, wo, wpg, wp, lng, lnb):
    row = lambda w: pl.BlockSpec((TM, w), lambda i: (i, 0))
    consts = (wm, bm, wa, wb, wc, wo, wpg, wp, lng, lnb)
    return pl.pallas_call(
        _merge_kernel,
        grid=(N_TOK // TM,),
        in_specs=[row(D_MODEL)] + [row(MOBA_W)] * 6 + [row(PLE_DIM)]
                 + [_const_spec(c.shape) for c in consts],
        out_specs=row(D_MODEL),
        out_shape=jax.ShapeDtypeStruct((N_TOK, D_MODEL), F32),
        compiler_params=pltpu.CompilerParams(
            dimension_semantics=("arbitrary",), vmem_limit_bytes=VMEM_LIMIT),
        name="merge",
    )(x2, ya, yb, yc, za, zb, zc, p2, *consts)


def _prep_proj_weights(w_in, w_uq, w_ukv):
    pts = np.cumsum([MOBA_W] * 4 + [MLA_Q_LORA, MLA_KV_LORA, MLA_ROPE, MLA_W] + [DIFF_W] * 4)
    a, cq, ckv, kr, bz, c = (w_in[:, :pts[3]], w_in[:, pts[3]:pts[4]], w_in[:, pts[4]:pts[5]],
                             w_in[:, pts[5]:pts[6]], w_in[:, pts[6]:pts[7]], w_in[:, pts[7]:])
    z_lo = jnp.zeros((D_MODEL, MLA_NOPE), F32)
    z_hi = jnp.zeros((D_MODEL, LANES - MLA_NOPE - MLA_ROPE), F32)
    kr128 = jnp.concatenate([z_lo, kr, z_hi], axis=-1)
    krs128 = jnp.concatenate([z_lo, _rot_half_cols(kr), z_hi], axis=-1)
    wcat = jnp.concatenate([a, cq, ckv, kr128, krs128, bz, c], axis=-1).astype(BF16)

    dq = MLA_NOPE + MLA_ROPE
    uq = w_uq.reshape(MLA_Q_LORA, MLA_HEADS, dq)
    pad_q = jnp.zeros((MLA_Q_LORA, MLA_HEADS, LANES - dq), F32)
    wq = jnp.concatenate([uq, pad_q], axis=-1)
    wqs = jnp.concatenate([jnp.zeros_like(uq[..., :MLA_NOPE]),
                           _rot_half_cols(uq[..., MLA_NOPE:]), pad_q], axis=-1)
    ukv = w_ukv.reshape(MLA_KV_LORA, MLA_HEADS, MLA_NOPE + MLA_V)
    wkn = jnp.concatenate([ukv[..., :MLA_NOPE],
                           jnp.zeros((MLA_KV_LORA, MLA_HEADS, LANES - MLA_NOPE), F32)], axis=-1)
    wv = ukv[..., MLA_NOPE:]
    flat = lambda w: w.reshape(w.shape[0], -1).astype(BF16)
    return wcat, flat(wq), flat(wqs), flat(wkn), flat(wv)


def kernel(x, p, w_in, mla_q_norm_g, mla_kv_norm_g, mla_w_uq, mla_w_ukv, diff_lambda, diff_subln_g, w_branch_a, w_branch_b, w_branch_c, w_merge, b_merge, w_out, ln_g, ln_b, w_ple_gate, w_ple):
    slopes_a, slopes_c = _alibi_slopes()
    cos_t, sin_t = _rope_tables()
    h = x.reshape(N_TOK, D_MODEL)
    for i in range(DEPTH):
        wcat, wq, wqs, wkn, wv = _prep_proj_weights(w_in[i], mla_w_uq[i], mla_w_ukv[i])
        (aq, ak, av, az, kmean, bq, bk, bv, bz, cq, ck, cv, cz) = _proj_call(
            h, wcat, wq, wqs, wkn, wv, mla_q_norm_g[i][None, :], mla_kv_norm_g[i][None, :],
            cos_t, sin_t)
        kmean = kmean.reshape(BATCH, N_KV_BLOCKS, MOBA_W)
        ya = _moba_call(slopes_a, aq, ak, av, kmean)
        yb = _mla_call(bq, bk, bv)
        lam_init = 0.8 - 0.6 * math.exp(-0.3 * i)
        yc = _diff_call(lam_init, slopes_c, cq, ck, cv, diff_lambda[i], diff_subln_g[i][None, :])
        bf = lambda w: w.astype(BF16)
        h = _merge_call(h, ya, yb, yc, az, bz, cz, p[i].reshape(N_TOK, PLE_DIM),
                        bf(w_merge[i]), b_merge[i][None, :], bf(w_branch_a[i]),
                        bf(w_branch_b[i]), bf(w_branch_c[i]), bf(w_out[i]), bf(w_ple_gate[i]),
                        bf(w_ple[i]), ln_g[i][None, :], ln_b[i][None, :])
    return h.reshape(BATCH, SEQ, D_MODEL)
```

```python
import functools
import math

import numpy as np
import jax
import jax.numpy as jnp
from jax import lax
from jax.experimental import pallas as pl
from jax.experimental.pallas import tpu as pltpu

D_MODEL = 1024
BATCH = 8
SEQ = 2048
DEPTH = 2
MOBA_HEADS = 8
HEAD_DIM = 64
MOBA_BLOCK = 256
MOBA_TOPK = 3
MLA_HEADS = 8
MLA_Q_LORA = 384
MLA_KV_LORA = 256
MLA_NOPE = 64
MLA_ROPE = 32
MLA_V = 64
ROPE_THETA = 10000.0
DIFF_HEADS = 4
DIFF_QK = 64
DIFF_V = 2 * DIFF_QK
PLE_DIM = 256
NORM_EPS = 1e-5
NEG = -1e30
N_BRANCH = 3
ALPHA = (2 * DEPTH) ** 0.25
LOG2E = math.log2(math.e)

MOBA_W = MOBA_HEADS * HEAD_DIM
MLA_W = MLA_HEADS * MLA_V
DIFF_W = DIFF_HEADS * DIFF_V
N_TOK = BATCH * SEQ
N_KV_BLOCKS = SEQ // MOBA_BLOCK

LANES = 128
VMEM_LIMIT = 54 * 1024 * 1024
TM = 256
TQ = 512
TK = 512
HALF = TQ // 2

OFF_A = 0
OFF_CQ = OFF_A + 4 * MOBA_W
OFF_CKV = OFF_CQ + MLA_Q_LORA
OFF_KR = OFF_CKV + MLA_KV_LORA
OFF_KRS = OFF_KR + LANES
OFF_BZ = OFF_KRS + LANES
OFF_C = OFF_BZ + MLA_W
W_CAT = OFF_C + 4 * DIFF_W

F32 = jnp.float32
BF16 = jnp.bfloat16


def _dot(a, b):
    return jnp.dot(a, b, preferred_element_type=F32)


def _dot_nt(a, b):
    return lax.dot_general(a, b, (((1,), (1,)), ((), ())), preferred_element_type=F32)


def _alibi_slopes():
    n = MOBA_HEADS + DIFF_HEADS
    s = 2.0 ** (-8.0 * (np.arange(n) + 1) / n)
    diff_idx = np.arange(DIFF_HEADS) * (n // DIFF_HEADS)
    moba_idx = np.setdiff1d(np.arange(n), diff_idx)
    return (jnp.asarray(s[moba_idx], dtype=F32), jnp.asarray(s[diff_idx], dtype=F32))


def _rope_tables():
    d = MLA_ROPE
    freqs = ROPE_THETA ** (-np.arange(0, d, 2, dtype=np.float32) / d)
    ang = np.arange(SEQ, dtype=np.float32)[:, None] * freqs[None, :]
    cos, sin = np.cos(ang), np.sin(ang)
    cos_t = np.zeros((SEQ, LANES), np.float32)
    sin_t = np.zeros((SEQ, LANES), np.float32)
    cos_t[:, :MLA_NOPE] = 1.0
    cos_t[:, MLA_NOPE:MLA_NOPE + d] = np.concatenate([cos, cos], axis=-1)
    sin_t[:, MLA_NOPE:MLA_NOPE + d] = np.concatenate([sin, sin], axis=-1)
    return jnp.asarray(cos_t), jnp.asarray(sin_t)


def _rot_half_cols(w):
    half = MLA_ROPE // 2
    return jnp.concatenate([-w[..., half:], w[..., :half]], axis=-1)


def _proj_kernel(x_ref, w_ref, wq_ref, wqs_ref, wkn_ref, wv_ref, gq_ref, gkv_ref,
                 cos_ref, sin_ref, ktab_ref,
                 aq_ref, ak_ref, avt_ref, az_ref, kmean_ref,
                 bq_ref, bk_ref, bvt_ref, bz_ref,
                 cq_ref, ck_ref, cvt_ref, cz_ref):
    xb = x_ref[...].astype(BF16)
    ktab = ktab_ref[...]
    ones_rows = jnp.where(lax.broadcasted_iota(jnp.int32, (ONES_ROWS, TM), 0) == 0,
                          1.0, 0.0).astype(BF16)

    def store_vt(vt_ref, v, dv):
        vt = v.T.astype(BF16)
        for h in range(v.shape[1] // dv):
            base = h * (dv + ONES_ROWS)
            vt_ref[base:base + dv, :] = vt[h * dv:(h + 1) * dv]
            vt_ref[base + dv:base + dv + ONES_ROWS, :] = ones_rows

    def store_k_aug(k_ref, k):
        for h in range(k.shape[1] // LANES):
            k_ref[:, 2 * h * LANES:(2 * h + 1) * LANES] = (
                k[:, h * LANES:(h + 1) * LANES].astype(BF16))
            k_ref[:, (2 * h + 1) * LANES:(2 * h + 2) * LANES] = ktab

    def proj(off, width):
        return _dot(xb, w_ref[:, off:off + width])

    aq_ref[...] = (proj(OFF_A, MOBA_W) * (HEAD_DIM ** -0.5 * LOG2E)).astype(BF16)
    ka = proj(OFF_A + MOBA_W, MOBA_W)
    store_k_aug(ak_ref, ka)
    kmean_ref[0] = jnp.mean(ka, axis=0, keepdims=True)
    store_vt(avt_ref, proj(OFF_A + 2 * MOBA_W, MOBA_W), HEAD_DIM)
    az_ref[...] = proj(OFF_A + 3 * MOBA_W, MOBA_W).astype(BF16)

    cos_t = cos_ref[...]
    sin_t = sin_ref[...]
    cq = proj(OFF_CQ, MLA_Q_LORA)
    cqn = cq * lax.rsqrt(jnp.mean(cq * cq, axis=-1, keepdims=True) + 1e-6) * gq_ref[...]
    cqn = cqn.astype(BF16)
    qa = _dot(cqn, wq_ref[...])
    qb = _dot(cqn, wqs_ref[...])
    qscale = (MLA_NOPE + MLA_ROPE) ** -0.5 * LOG2E
    cos_q = cos_t * qscale
    sin_q = sin_t * qscale
    for h in range(MLA_HEADS):
        sl = slice(h * LANES, (h + 1) * LANES)
        bq_ref[:, sl] = (qa[:, sl] * cos_q + qb[:, sl] * sin_q).astype(BF16)

    ckv = proj(OFF_CKV, MLA_KV_LORA)
    ckvn = ckv * lax.rsqrt(jnp.mean(ckv * ckv, axis=-1, keepdims=True) + 1e-6) * gkv_ref[...]
    ckvn = ckvn.astype(BF16)
    kn = _dot(ckvn, wkn_ref[...])
    krot = proj(OFF_KR, LANES) * cos_t + proj(OFF_KRS, LANES) * sin_t
    for h in range(MLA_HEADS):
        sl = slice(h * LANES, (h + 1) * LANES)
        bk_ref[:, sl] = (kn[:, sl] + krot).astype(BF16)
    store_vt(bvt_ref, _dot(ckvn, wv_ref[...]), MLA_V)
    bz_ref[...] = proj(OFF_BZ, MLA_W).astype(BF16)

    cq_ref[...] = (proj(OFF_C, DIFF_W) * (DIFF_QK ** -0.5 * LOG2E)).astype(BF16)
    store_k_aug(ck_ref, proj(OFF_C + DIFF_W, DIFF_W))
    store_vt(cvt_ref, proj(OFF_C + 2 * DIFF_W, DIFF_W), DIFF_V)
    cz_ref[...] = proj(OFF_C + 3 * DIFF_W, DIFF_W).astype(BF16)


def _const_spec(shape):
    nd = len(shape)
    return pl.BlockSpec(shape, lambda i: (0,) * nd)


def _proj_call(x2, wcat, wq, wqs, wkn, wv, gq, gkv, cos_t, sin_t, ktab):
    n_tiles = N_TOK // TM
    pos_tiles = SEQ // TM
    row = lambda w: pl.BlockSpec((TM, w), lambda i: (i, 0))
    tab = pl.BlockSpec((TM, LANES), lambda i: (i % pos_tiles, 0))
    out_shapes = []
    out_specs = []

    def add(width):
        out_shapes.append(jax.ShapeDtypeStruct((N_TOK, width), BF16))
        out_specs.append(row(width))

    def add_vt(heads, dv):
        rows = heads * (dv + ONES_ROWS)
        out_shapes.append(jax.ShapeDtypeStruct((BATCH, rows, SEQ), BF16))
        out_specs.append(pl.BlockSpec((None, rows, TM),
                                      lambda i: (i // pos_tiles, 0, i % pos_tiles)))

    add(MOBA_W); add(2 * MOBA_W); add_vt(MOBA_HEADS, HEAD_DIM); add(MOBA_W)
    out_shapes.append(jax.ShapeDtypeStruct((n_tiles, 1, MOBA_W), F32))
    out_specs.append(pl.BlockSpec((1, 1, MOBA_W), lambda i: (i, 0, 0)))
    add(MLA_HEADS * LANES); add(MLA_HEADS * LANES); add_vt(MLA_HEADS, MLA_V); add(MLA_W)
    add(DIFF_W); add(2 * DIFF_W); add_vt(DIFF_HEADS, DIFF_V); add(DIFF_W)
    return pl.pallas_call(
        _proj_kernel,
        grid=(n_tiles,),
        in_specs=[row(D_MODEL), _const_spec(wcat.shape), _const_spec(wq.shape),
                  _const_spec(wqs.shape), _const_spec(wkn.shape), _const_spec(wv.shape),
                  _const_spec(gq.shape), _const_spec(gkv.shape), tab, tab, tab],
        out_specs=out_specs,
        out_shape=out_shapes,
        compiler_params=pltpu.CompilerParams(
            dimension_semantics=("arbitrary",), vmem_limit_bytes=VMEM_LIMIT),
        name="proj",
    )(x2, wcat, wq, wqs, wkn, wv, gq, gkv, cos_t, sin_t, ktab)


AUG_POS = 8
ONES_ROWS = 16


def _split3(v):
    hi = v.astype(BF16).astype(F32)
    mid = (v - hi).astype(BF16).astype(F32)
    lo = (v - hi - mid).astype(BF16).astype(F32)
    return hi, mid, lo


def _key_aug_table():
    kpos = np.arange(SEQ)
    t = np.zeros((SEQ, LANES), np.float32)
    t[kpos, kpos // MOBA_BLOCK] = 1.0
    t[:, AUG_POS:AUG_POS + 3] = (kpos // 256)[:, None]
    t[:, AUG_POS + 3:AUG_POS + 6] = (kpos % 256)[:, None]
    return jnp.asarray(t).astype(BF16)


def _query_aug_rows(slopes, group):
    s2 = slopes * LOG2E
    pieces = jnp.stack(_split3(s2 * 256.0) + _split3(s2), axis=-1)
    rows = jnp.zeros((slopes.shape[0], LANES), F32).at[:, AUG_POS:AUG_POS + 6].set(pieces)
    rows = rows.reshape(slopes.shape[0] // group, group, LANES)
    return jnp.pad(rows, ((0, 0), (0, 8 - group), (0, 0)))


def _flash_chains(q_list, k_tile, vt_tile, ti):
    n = len(q_list)
    rng = range(n)
    key_i = lax.broadcasted_iota(jnp.int32, (TK, HALF), 0)
    qry_i = lax.broadcasted_iota(jnp.int32, (TK, HALF), 1)
    tri_lo = (key_i <= qry_i)[:HALF]
    tri_hi = key_i <= qry_i + HALF

    d0 = pl.multiple_of(ti * TK, TK)
    q_lo = [q_list[c][:HALF] for c in rng]
    q_hi = [q_list[c][HALF:] for c in rng]
    s_lo = [jnp.where(tri_lo, _dot_nt(k_tile(c, d0, HALF), q_lo[c]), NEG) for c in rng]
    s_hi = [jnp.where(tri_hi, _dot_nt(k_tile(c, d0, TK), q_hi[c]), NEG) for c in rng]
    m_lo = [jnp.max(s_lo[c], axis=0, keepdims=True) for c in rng]
    m_hi = [jnp.max(s_hi[c], axis=0, keepdims=True) for c in rng]
    p_lo = [jnp.exp2(s_lo[c] - m_lo[c]).astype(BF16) for c in rng]
    p_hi = [jnp.exp2(s_hi[c] - m_hi[c]).astype(BF16) for c in rng]
    acc_lo = [_dot(vt_tile(c, d0, HALF), p_lo[c]) for c in rng]
    acc_hi = [_dot(vt_tile(c, d0, TK), p_hi[c]) for c in rng]
    m = [jnp.concatenate([m_lo[c], m_hi[c]], axis=1) for c in rng]
    acc = [jnp.concatenate([acc_lo[c], acc_hi[c]], axis=1) for c in rng]

    def body(j, carry):
        m, acc = carry
        start = pl.multiple_of(j * TK, TK)
        s = [_dot_nt(k_tile(c, start, TK), q_list[c]) for c in rng]
        m_new = [jnp.maximum(m[c], jnp.max(s[c], axis=0, keepdims=True)) for c in rng]
        a = [jnp.exp2(m[c] - m_new[c]) for c in rng]
        p = [jnp.exp2(s[c] - m_new[c]).astype(BF16) for c in rng]
        acc = [a[c] * acc[c] + _dot(vt_tile(c, start, TK), p[c]) for c in rng]
        return tuple(m_new), tuple(acc)

    m, acc = lax.fori_loop(0, ti, body, (tuple(m), tuple(acc)))
    return acc


MOBA_NV = HEAD_DIM + ONES_ROWS


def _moba_kernel(qaug_ref, q_ref, k_ref, vt_ref, kmean_ref, o_ref):
    ti = pl.program_id(2)
    lane_q = lax.broadcasted_iota(jnp.int32, (TQ, LANES), 1)
    q2 = q_ref[...]
    km = jnp.concatenate(
        [kmean_ref[...], jnp.zeros((LANES - N_KV_BLOCKS, LANES), F32)], axis=0).astype(BF16)
    lane_h = lax.broadcasted_iota(jnp.int32, (HALF, LANES), 1)
    blk = lax.broadcasted_iota(jnp.int32, (N_KV_BLOCKS, HALF), 0)
    pad_rows = jnp.zeros((LANES - N_KV_BLOCKS, HALF), F32)
    q_list = []
    for hh in range(2):
        in_head = (lane_q // HEAD_DIM) == hh
        qm = jnp.where(in_head, q2, jnp.zeros_like(q2))
        gt = _dot_nt(km, qm)
        qx = []
        for half in range(2):
            qb = 2 * ti + half
            g = gt[:N_KV_BLOCKS, half * HALF:(half + 1) * HALF]
            cnt = jnp.zeros((N_KV_BLOCKS, HALF), F32)
            for mm in range(N_KV_BLOCKS):
                gm = g[mm:mm + 1, :]
                beats = (gm > g) | ((gm == g) & (mm < blk))
                cnt = cnt + jnp.where(beats, jnp.where(mm < qb, 1.0, 0.0), 0.0)
            attend = ((blk < qb) & (cnt < MOBA_TOPK)) | (blk == qb)
            selb_t = jnp.where(attend, 0.0, NEG)
            selb = jnp.concatenate([selb_t, pad_rows], axis=0).T
            qx.append(jnp.where(lane_h < N_KV_BLOCKS, selb, qaug_ref[hh:hh + 1, :]).astype(BF16))
        q_list.append(jnp.concatenate([qm, jnp.concatenate(qx, axis=0)], axis=1))

    acc = _flash_chains(
        q_list,
        lambda c, st, size: k_ref[pl.ds(st, size), :],
        lambda c, st, size: vt_ref[c * MOBA_NV:(c + 1) * MOBA_NV, pl.ds(st, size)],
        ti)
    out_t = jnp.concatenate(
        [a[:HEAD_DIM] / a[HEAD_DIM:HEAD_DIM + 1] for a in acc], axis=0)
    o_ref[...] = out_t.T.astype(BF16)


def _moba_call(qaug, q, k, vt, kmean):
    nq = SEQ // TQ
    return pl.pallas_call(
        _moba_kernel,
        grid=(BATCH, MOBA_HEADS // 2, nq),
        in_specs=[
            pl.BlockSpec((None, 8, LANES), lambda b, h, i: (h, 0, 0)),
            pl.BlockSpec((TQ, LANES), lambda b, h, i: (b * nq + i, h)),
            pl.BlockSpec((SEQ, 2 * LANES), lambda b, h, i: (b, h)),
            pl.BlockSpec((None, 2 * MOBA_NV, SEQ), lambda b, h, i: (b, h, 0)),
            pl.BlockSpec((None, N_KV_BLOCKS, LANES), lambda b, h, i: (b, 0, h)),
        ],
        out_specs=pl.BlockSpec((TQ, LANES), lambda b, h, i: (b * nq + i, h)),
        out_shape=jax.ShapeDtypeStruct((N_TOK, MOBA_W), BF16),
        compiler_params=pltpu.CompilerParams(
            dimension_semantics=("arbitrary", "arbitrary", "arbitrary"),
            vmem_limit_bytes=VMEM_LIMIT),
        name="moba",
    )(qaug, q, k, vt, kmean)


MLA_GROUP = 4


MLA_NV = MLA_V + ONES_ROWS


def _mla_kernel(q_ref, k_ref, vt_ref, o_ref):
    ti = pl.program_id(2)
    q_list = [q_ref[:, c * LANES:(c + 1) * LANES] for c in range(MLA_GROUP)]
    acc = _flash_chains(
        q_list,
        lambda c, st, size: k_ref[pl.ds(st, size), c * LANES:(c + 1) * LANES],
        lambda c, st, size: vt_ref[c * MLA_NV:(c + 1) * MLA_NV, pl.ds(st, size)],
        ti)
    for pr in range(MLA_GROUP // 2):
        out_t = jnp.concatenate(
            [a[:MLA_V] / a[MLA_V:MLA_V + 1] for a in acc[2 * pr:2 * pr + 2]], axis=0)
        o_ref[:, pr * LANES:(pr + 1) * LANES] = out_t.T.astype(BF16)


def _mla_call(q, k, vt):
    nq = SEQ // TQ
    g = MLA_GROUP
    return pl.pallas_call(
        _mla_kernel,
        grid=(BATCH, MLA_HEADS // g, nq),
        in_specs=[
            pl.BlockSpec((TQ, g * LANES), lambda b, h, i: (b * nq + i, h)),
            pl.BlockSpec((SEQ, g * LANES), lambda b, h, i: (b, h)),
            pl.BlockSpec((None, g * MLA_NV, SEQ), lambda b, h, i: (b, h, 0)),
        ],
        out_specs=pl.BlockSpec((TQ, g * MLA_V), lambda b, h, i: (b * nq + i, h)),
        out_shape=jax.ShapeDtypeStruct((N_TOK, MLA_W), BF16),
        compiler_params=pltpu.CompilerParams(
            dimension_semantics=("arbitrary", "arbitrary", "arbitrary"),
            vmem_limit_bytes=VMEM_LIMIT),
        name="mla",
    )(q, k, vt)


DIFF_GROUP = 2


DIFF_NV = DIFF_V + ONES_ROWS


def _diff_kernel(lam_init, qaug_ref, q_ref, k_ref, vt_ref, lam_ref, g_ref, o_ref):
    ti = pl.program_id(2)
    lane_q = lax.broadcasted_iota(jnp.int32, (TQ, LANES), 1)
    q_list = []
    for hh in range(DIFF_GROUP):
        q2 = q_ref[:, hh * LANES:(hh + 1) * LANES]
        qx = jnp.broadcast_to(qaug_ref[hh:hh + 1, :], (TQ, LANES)).astype(BF16)
        for c in range(2):
            qm = jnp.where((lane_q // DIFF_QK) == c, q2, jnp.zeros_like(q2))
            q_list.append(jnp.concatenate([qm, qx], axis=1))

    acc = _flash_chains(
        q_list,
        lambda c, st, size: k_ref[pl.ds(st, size), (c // 2) * 2 * LANES:(c // 2 + 1) * 2 * LANES],
        lambda c, st, size: vt_ref[(c // 2) * DIFF_NV:(c // 2 + 1) * DIFF_NV, pl.ds(st, size)],
        ti)

    lf = lam_ref[...]
    e1 = jnp.exp(jnp.sum(lf[0:1, :] * lf[1:2, :], axis=-1, keepdims=True))
    e2 = jnp.exp(jnp.sum(lf[2:3, :] * lf[3:4, :], axis=-1, keepdims=True))
    lam = e1 - e2 + lam_init
    for hh in range(DIFF_GROUP):
        a0, a1 = acc[2 * hh], acc[2 * hh + 1]
        o0 = a0[:DIFF_V] / a0[DIFF_V:DIFF_V + 1]
        o1 = a1[:DIFF_V] / a1[DIFF_V:DIFF_V + 1]
        y = (o0 - lam * o1).T
        y = y * lax.rsqrt(jnp.mean(y * y, axis=-1, keepdims=True) + 1e-5) * g_ref[...]
        o_ref[:, hh * LANES:(hh + 1) * LANES] = (y * (1.0 - lam_init)).astype(BF16)


def _diff_call(lam_init, qaug, q, k, vt, lam_p, subln_g):
    nq = SEQ // TQ
    g = DIFF_GROUP
    return pl.pallas_call(
        functools.partial(_diff_kernel, lam_init),
        grid=(BATCH, DIFF_HEADS // g, nq),
        in_specs=[
            pl.BlockSpec((None, 8, LANES), lambda b, h, i: (h, 0, 0)),
            pl.BlockSpec((TQ, g * LANES), lambda b, h, i: (b * nq + i, h)),
            pl.BlockSpec((SEQ, g * 2 * LANES), lambda b, h, i: (b, h)),
            pl.BlockSpec((None, g * DIFF_NV, SEQ), lambda b, h, i: (b, h, 0)),
            pl.BlockSpec(lam_p.shape, lambda b, h, i: (0, 0)),
            pl.BlockSpec(subln_g.shape, lambda b, h, i: (0, 0)),
        ],
        out_specs=pl.BlockSpec((TQ, g * LANES), lambda b, h, i: (b * nq + i, h)),
        out_shape=jax.ShapeDtypeStruct((N_TOK, DIFF_W), BF16),
        compiler_params=pltpu.CompilerParams(
            dimension_semantics=("arbitrary", "arbitrary", "arbitrary"),
            vmem_limit_bytes=VMEM_LIMIT),
        name="diff",
    )(qaug, q, k, vt, lam_p, subln_g)


def _merge_kernel(x_ref, ya_ref, yb_ref, yc_ref, za_ref, zb_ref, zc_ref, p_ref,
                  wm_ref, bm_ref, wa_ref, wb_ref, wc_ref, wo_ref, wpg_ref, wp_ref,
                  lng_ref, lnb_ref, o_ref):
    x = x_ref[...]
    xb = x.astype(BF16)
    merged = jnp.zeros((TM, D_MODEL), F32)
    branches = ((ya_ref, za_ref, wa_ref), (yb_ref, zb_ref, wb_ref), (yc_ref, zc_ref, wc_ref))
    for i, (y_ref, z_ref, w_ref) in enumerate(branches):
        sl = slice(i * D_MODEL, (i + 1) * D_MODEL)
        gate = jax.nn.sigmoid(_dot(xb, wm_ref[:, sl]) + bm_ref[:, sl])
        z = z_ref[...].astype(F32)
        y = y_ref[...].astype(F32) * (z * jax.nn.sigmoid(z))
        merged = merged + gate * _dot(y.astype(BF16), w_ref[...])
    out = _dot(merged.astype(BF16), wo_ref[...])
    r = ALPHA * x + out
    ple = _dot(p_ref[...].astype(BF16), wp_ref[...])
    r = r + jax.nn.sigmoid(_dot(r.astype(BF16), wpg_ref[...])) * ple
    mu = jnp.mean(r, axis=-1, keepdims=True)
    d = r - mu
    var = jnp.mean(d * d, axis=-1, keepdims=True)
    o_ref[...] = d * lax.rsqrt(var + NORM_EPS) * lng_ref[...] + lnb_ref[...]


def _merge_call(x2, ya, yb, yc, za, zb, zc, p2, wm, bm, wa, wb, wc, wo, wpg, wp, lng, lnb):
    row = lambda w: pl.BlockSpec((TM, w), lambda i: (i, 0))
    consts = (wm, bm, wa, wb, wc, wo, wpg, wp, lng, lnb)
    return pl.pallas_call(
        _merge_kernel,
        grid=(N_TOK // TM,),
        in_specs=[row(D_MODEL)] + [row(MOBA_W)] * 6 + [row(PLE_DIM)]
                 + [_const_spec(c.shape) for c in consts],
        out_specs=row(D_MODEL),
        out_shape=jax.ShapeDtypeStruct((N_TOK, D_MODEL), F32),
        compiler_params=pltpu.CompilerParams(
            dimension_semantics=("arbitrary",), vmem_limit_bytes=VMEM_LIMIT),
        name="merge",
    )(x2, ya, yb, yc, za, zb, zc, p2, *consts)


def _prep_proj_weights(w_in, w_uq, w_ukv):
    pts = np.cumsum([MOBA_W] * 4 + [MLA_Q_LORA, MLA_KV_LORA, MLA_ROPE, MLA_W] + [DIFF_W] * 4)
    a, cq, ckv, kr, bz, c = (w_in[:, :pts[3]], w_in[:, pts[3]:pts[4]], w_in[:, pts[4]:pts[5]],
                             w_in[:, pts[5]:pts[6]], w_in[:, pts[6]:pts[7]], w_in[:, pts[7]:])
    z_lo = jnp.zeros((D_MODEL, MLA_NOPE), F32)
    z_hi = jnp.zeros((D_MODEL, LANES - MLA_NOPE - MLA_ROPE), F32)
    kr128 = jnp.concatenate([z_lo, kr, z_hi], axis=-1)
    krs128 = jnp.concatenate([z_lo, _rot_half_cols(kr), z_hi], axis=-1)
    wcat = jnp.concatenate([a, cq, ckv, kr128, krs128, bz, c], axis=-1).astype(BF16)

    dq = MLA_NOPE + MLA_ROPE
    uq = w_uq.reshape(MLA_Q_LORA, MLA_HEADS, dq)
    pad_q = jnp.zeros((MLA_Q_LORA, MLA_HEADS, LANES - dq), F32)
    wq = jnp.concatenate([uq, pad_q], axis=-1)
    wqs = jnp.concatenate([jnp.zeros_like(uq[..., :MLA_NOPE]),
                           _rot_half_cols(uq[..., MLA_NOPE:]), pad_q], axis=-1)
    ukv = w_ukv.reshape(MLA_KV_LORA, MLA_HEADS, MLA_NOPE + MLA_V)
    wkn = jnp.concatenate([ukv[..., :MLA_NOPE],
                           jnp.zeros((MLA_KV_LORA, MLA_HEADS, LANES - MLA_NOPE), F32)], axis=-1)
    wv = ukv[..., MLA_NOPE:]
    flat = lambda w: w.reshape(w.shape[0], -1).astype(BF16)
    return wcat, flat(wq), flat(wqs), flat(wkn), flat(wv)


def kernel(x, p, w_in, mla_q_norm_g, mla_kv_norm_g, mla_w_uq, mla_w_ukv, diff_lambda, diff_subln_g, w_branch_a, w_branch_b, w_branch_c, w_merge, b_merge, w_out, ln_g, ln_b, w_ple_gate, w_ple):
    slopes_a, slopes_c = _alibi_slopes()
    qaug_a = _query_aug_rows(slopes_a, 2)
    qaug_c = _query_aug_rows(slopes_c, DIFF_GROUP)
    ktab = _key_aug_table()
    cos_t, sin_t = _rope_tables()
    h = x.reshape(N_TOK, D_MODEL)
    for i in range(DEPTH):
        wcat, wq, wqs, wkn, wv = _prep_proj_weights(w_in[i], mla_w_uq[i], mla_w_ukv[i])
        (aq, ak, avt, az, kmean, bq, bk, bvt, bz, cq, ck, cvt, cz) = _proj_call(
            h, wcat, wq, wqs, wkn, wv, mla_q_norm_g[i][None, :], mla_kv_norm_g[i][None, :],
            cos_t, sin_t, ktab)
        kmean = kmean.reshape(BATCH, N_KV_BLOCKS, MOBA_W)
        ya = _moba_call(qaug_a, aq, ak, avt, kmean)
        yb = _mla_call(bq, bk, bvt)
        lam_init = 0.8 - 0.6 * math.exp(-0.3 * i)
        yc = _diff_call(lam_init, qaug_c, cq, ck, cvt, diff_lambda[i], diff_subln_g[i][None, :])
        bf = lambda w: w.astype(BF16)
        h = _merge_call(h, ya, yb, yc, az, bz, cz, p[i].reshape(N_TOK, PLE_DIM),
                        bf(w_merge[i]), b_merge[i][None, :], bf(w_branch_a[i]),
                        bf(w_branch_b[i]), bf(w_branch_c[i]), bf(w_out[i]), bf(w_ple_gate[i]),
                        bf(w_ple[i]), ln_g[i][None, :], ln_b[i][None, :])
    return h.reshape(BATCH, SEQ, D_MODEL)
```

```python
import functools
import math

import numpy as np
import jax
import jax.numpy as jnp
from jax import lax
from jax.experimental import pallas as pl
from jax.experimental.pallas import tpu as pltpu

D_MODEL = 1024
BATCH = 8
SEQ = 2048
DEPTH = 2
MOBA_HEADS = 8
HEAD_DIM = 64
MOBA_BLOCK = 256
MOBA_TOPK = 3
MLA_HEADS = 8
MLA_Q_LORA = 384
MLA_KV_LORA = 256
MLA_NOPE = 64
MLA_ROPE = 32
MLA_V = 64
ROPE_THETA = 10000.0
DIFF_HEADS = 4
DIFF_QK = 64
DIFF_V = 2 * DIFF_QK
PLE_DIM = 256
NORM_EPS = 1e-5
NEG = -1e30
N_BRANCH = 3
ALPHA = (2 * DEPTH) ** 0.25
LOG2E = math.log2(math.e)

MOBA_W = MOBA_HEADS * HEAD_DIM
MLA_W = MLA_HEADS * MLA_V
DIFF_W = DIFF_HEADS * DIFF_V
N_TOK = BATCH * SEQ
N_KV_BLOCKS = SEQ // MOBA_BLOCK

LANES = 128
VMEM_LIMIT = 54 * 1024 * 1024
TM = 256
TM_MERGE = 512
TM_SUB = 256
TQ = 512
TK = 512
HALF = TQ // 2

OFF_A = 0
OFF_CQ = OFF_A + 4 * MOBA_W
OFF_CKV = OFF_CQ + MLA_Q_LORA
OFF_KR = OFF_CKV + MLA_KV_LORA
OFF_KRS = OFF_KR + LANES
OFF_BZ = OFF_KRS + LANES
OFF_C = OFF_BZ + MLA_W
W_CAT = OFF_C + 4 * DIFF_W

F32 = jnp.float32
BF16 = jnp.bfloat16


def _dot(a, b):
    return jnp.dot(a, b, preferred_element_type=F32)


def _dot_nt(a, b):
    return lax.dot_general(a, b, (((1,), (1,)), ((), ())), preferred_element_type=F32)


def _alibi_slopes():
    n = MOBA_HEADS + DIFF_HEADS
    s = 2.0 ** (-8.0 * (np.arange(n) + 1) / n)
    diff_idx = np.arange(DIFF_HEADS) * (n // DIFF_HEADS)
    moba_idx = np.setdiff1d(np.arange(n), diff_idx)
    return (jnp.asarray(s[moba_idx], dtype=F32), jnp.asarray(s[diff_idx], dtype=F32))


def _rope_tables():
    d = MLA_ROPE
    freqs = ROPE_THETA ** (-np.arange(0, d, 2, dtype=np.float32) / d)
    ang = np.arange(SEQ, dtype=np.float32)[:, None] * freqs[None, :]
    cos, sin = np.cos(ang), np.sin(ang)
    cos_t = np.zeros((SEQ, LANES), np.float32)
    sin_t = np.zeros((SEQ, LANES), np.float32)
    cos_t[:, :MLA_NOPE] = 1.0
    cos_t[:, MLA_NOPE:MLA_NOPE + d] = np.concatenate([cos, cos], axis=-1)
    sin_t[:, MLA_NOPE:MLA_NOPE + d] = np.concatenate([sin, sin], axis=-1)
    return jnp.asarray(cos_t), jnp.asarray(sin_t)


def _rot_half_cols(w):
    half = MLA_ROPE // 2
    return jnp.concatenate([-w[..., half:], w[..., :half]], axis=-1)


def _proj_kernel(x_ref, w_ref, wq_ref, wqs_ref, wkn_ref, wv_ref, gq_ref, gkv_ref,
                 cos_ref, sin_ref, ktab_ref,
                 aq_ref, ak_ref, avt_ref, az_ref, kmean_ref,
                 bq_ref, bk_ref, bvt_ref, bz_ref,
                 cq_ref, ck_ref, cvt_ref, cz_ref):
    xb = x_ref[...].astype(BF16)
    ktab = ktab_ref[...]
    ones_rows = jnp.where(lax.broadcasted_iota(jnp.int32, (ONES_ROWS, TM), 0) == 0,
                          1.0, 0.0).astype(BF16)

    def store_vt(vt_ref, v, dv):
        vt = v.T.astype(BF16)
        for h in range(v.shape[1] // dv):
            base = h * (dv + ONES_ROWS)
            vt_ref[base:base + dv, :] = vt[h * dv:(h + 1) * dv]
            vt_ref[base + dv:base + dv + ONES_ROWS, :] = ones_rows

    def store_k_aug(k_ref, k):
        for h in range(k.shape[1] // LANES):
            k_ref[:, 2 * h * LANES:(2 * h + 1) * LANES] = (
                k[:, h * LANES:(h + 1) * LANES].astype(BF16))
            k_ref[:, (2 * h + 1) * LANES:(2 * h + 2) * LANES] = ktab

    def proj(off, width):
        return _dot(xb, w_ref[:, off:off + width])

    cos_t = cos_ref[...]
    sin_t = sin_ref[...]
    cq = proj(OFF_CQ, MLA_Q_LORA)
    ckv = proj(OFF_CKV, MLA_KV_LORA)
    krot = proj(OFF_KR, LANES) * cos_t + proj(OFF_KRS, LANES) * sin_t
    cqn = cq * lax.rsqrt(jnp.mean(cq * cq, axis=-1, keepdims=True) + 1e-6) * gq_ref[...]
    cqn = cqn.astype(BF16)
    ckvn = ckv * lax.rsqrt(jnp.mean(ckv * ckv, axis=-1, keepdims=True) + 1e-6) * gkv_ref[...]
    ckvn = ckvn.astype(BF16)

    aq_ref[...] = (proj(OFF_A, MOBA_W) * (HEAD_DIM ** -0.5 * LOG2E)).astype(BF16)
    ka = proj(OFF_A + MOBA_W, MOBA_W)
    store_k_aug(ak_ref, ka)
    kmean_ref[0] = jnp.mean(ka, axis=0, keepdims=True)
    store_vt(avt_ref, proj(OFF_A + 2 * MOBA_W, MOBA_W), HEAD_DIM)
    az_ref[...] = proj(OFF_A + 3 * MOBA_W, MOBA_W).astype(BF16)

    qa = _dot(cqn, wq_ref[...])
    qb = _dot(cqn, wqs_ref[...])
    qscale = (MLA_NOPE + MLA_ROPE) ** -0.5 * LOG2E
    cos_q = cos_t * qscale
    sin_q = sin_t * qscale
    for h in range(MLA_HEADS):
        sl = slice(h * LANES, (h + 1) * LANES)
        bq_ref[:, sl] = (qa[:, sl] * cos_q + qb[:, sl] * sin_q).astype(BF16)
    kn = _dot(ckvn, wkn_ref[...])
    for h in range(MLA_HEADS):
        sl = slice(h * LANES, (h + 1) * LANES)
        bk_ref[:, sl] = (kn[:, sl] + krot).astype(BF16)
    store_vt(bvt_ref, _dot(ckvn, wv_ref[...]), MLA_V)
    bz_ref[...] = proj(OFF_BZ, MLA_W).astype(BF16)

    cq_ref[...] = (proj(OFF_C, DIFF_W) * (DIFF_QK ** -0.5 * LOG2E)).astype(BF16)
    store_k_aug(ck_ref, proj(OFF_C + DIFF_W, DIFF_W))
    store_vt(cvt_ref, proj(OFF_C + 2 * DIFF_W, DIFF_W), DIFF_V)
    cz_ref[...] = proj(OFF_C + 3 * DIFF_W, DIFF_W).astype(BF16)


def _const_spec(shape):
    nd = len(shape)
    return pl.BlockSpec(shape, lambda i: (0,) * nd, pipeline_mode=pl.Buffered(1))


def _proj_call(x2, wcat, wq, wqs, wkn, wv, gq, gkv, cos_t, sin_t, ktab):
    n_tiles = N_TOK // TM
    pos_tiles = SEQ // TM
    row = lambda w: pl.BlockSpec((TM, w), lambda i: (i, 0))
    tab = pl.BlockSpec((TM, LANES), lambda i: (i % pos_tiles, 0))
    out_shapes = []
    out_specs = []

    def add(width):
        out_shapes.append(jax.ShapeDtypeStruct((N_TOK, width), BF16))
        out_specs.append(row(width))

    def add_vt(heads, dv):
        rows = heads * (dv + ONES_ROWS)
        out_shapes.append(jax.ShapeDtypeStruct((BATCH, rows, SEQ), BF16))
        out_specs.append(pl.BlockSpec((None, rows, TM),
                                      lambda i: (i // pos_tiles, 0, i % pos_tiles)))

    add(MOBA_W); add(2 * MOBA_W); add_vt(MOBA_HEADS, HEAD_DIM); add(MOBA_W)
    out_shapes.append(jax.ShapeDtypeStruct((n_tiles, 1, MOBA_W), F32))
    out_specs.append(pl.BlockSpec((1, 1, MOBA_W), lambda i: (i, 0, 0)))
    add(MLA_HEADS * LANES); add(MLA_HEADS * LANES); add_vt(MLA_HEADS, MLA_V); add(MLA_W)
    add(DIFF_W); add(2 * DIFF_W); add_vt(DIFF_HEADS, DIFF_V); add(DIFF_W)
    return pl.pallas_call(
        _proj_kernel,
        grid=(n_tiles,),
        in_specs=[row(D_MODEL), _const_spec(wcat.shape), _const_spec(wq.shape),
                  _const_spec(wqs.shape), _const_spec(wkn.shape), _const_spec(wv.shape),
                  _const_spec(gq.shape), _const_spec(gkv.shape), tab, tab, tab],
        out_specs=out_specs,
        out_shape=out_shapes,
        compiler_params=pltpu.CompilerParams(
            dimension_semantics=("arbitrary",), vmem_limit_bytes=VMEM_LIMIT),
        name="proj",
    )(x2, wcat, wq, wqs, wkn, wv, gq, gkv, cos_t, sin_t, ktab)


AUG_POS = 8
ONES_ROWS = 16


def _split3(v):
    hi = v.astype(BF16).astype(F32)
    mid = (v - hi).astype(BF16).astype(F32)
    lo = (v - hi - mid).astype(BF16).astype(F32)
    return hi, mid, lo


def _key_aug_table():
    kpos = np.arange(SEQ)
    t = np.zeros((SEQ, LANES), np.float32)
    t[kpos, kpos // MOBA_BLOCK] = 1.0
    t[:, AUG_POS:AUG_POS + 3] = (kpos // 256)[:, None]
    t[:, AUG_POS + 3:AUG_POS + 6] = (kpos % 256)[:, None]
    return jnp.asarray(t).astype(BF16)


def _query_aug_rows(slopes, group):
    s2 = slopes * LOG2E
    pieces = jnp.stack(_split3(s2 * 256.0) + _split3(s2), axis=-1)
    rows = jnp.zeros((slopes.shape[0], LANES), F32).at[:, AUG_POS:AUG_POS + 6].set(pieces)
    rows = rows.reshape(slopes.shape[0] // group, group, LANES)
    return jnp.pad(rows, ((0, 0), (0, 8 - group), (0, 0)))


def _flash_chains(q_list, k_tile, vt_tile, ti, q_first=None):
    n = len(q_list)
    key_i = lax.broadcasted_iota(jnp.int32, (TK, HALF), 0)
    qry_i = lax.broadcasted_iota(jnp.int32, (TK, HALF), 1)
    tri_lo = (key_i <= qry_i)[:HALF]
    tri_hi = key_i <= qry_i + HALF

    q_cache = {}

    def q_sub(sc):
        if sc // 2 not in q_cache:
            q = q_list[sc // 2]
            q_cache[sc // 2] = q() if callable(q) else q
        return q_cache[sc // 2][(sc % 2) * HALF:(sc % 2 + 1) * HALF]

    d0 = ti * TK
    items = []
    for sc in range(2 * n):
        items.append((sc, d0, TK, tri_hi) if sc % 2 else (sc, d0, HALF, tri_lo))
    for j in range(ti):
        items += [(sc, j * TK, TK, None) for sc in range(2 * n)]

    m = [None] * (2 * n)
    acc = [None] * (2 * n)
    live = {}

    def scores(i):
        sc, start, size, mask = items[i]
        if q_first is not None and m[sc] is None and sc % 2 == 0:
            q = q_first[sc // 2][:HALF]
        else:
            q = q_sub(sc)
        s = _dot_nt(k_tile(sc // 2, start, size), q)
        if mask is not None:
            s = jnp.where(mask, s, NEG)
        smax = jnp.max(s, axis=0, keepdims=True)
        m_new = smax if m[sc] is None else jnp.maximum(m[sc], smax)
        a = None if m[sc] is None else jnp.exp2(m[sc] - m_new)
        m[sc] = m_new
        live[i] = (s, m_new, a)

    def probs(i):
        s, m_new, a = live[i]
        live[i] = (jnp.exp2(s - m_new).astype(BF16), a)

    def values(i):
        sc, start, size, _ = items[i]
        p, a = live.pop(i)
        pv = _dot(vt_tile(sc // 2, start, size), p)
        acc[sc] = pv if a is None else a * acc[sc] + pv

    for r in range(0, len(items), 2 * n):
        rnd = range(r, r + 2 * n)
        for i in rnd:
            scores(i)
        for i in rnd:
            probs(i)
        for i in rnd:
            values(i)
    return [jnp.concatenate([acc[2 * c], acc[2 * c + 1]], axis=1) for c in range(n)]


def _per_query_tile(ti, fn):
    for t in range(SEQ // TQ):
        pl.when(ti == t)(functools.partial(fn, t))


MOBA_NV = HEAD_DIM + ONES_ROWS
MOBA_GROUP = 4


def _moba_kernel(qaug_ref, q_ref, k_ref, vt_ref, kmean_ref, o_ref):
    ti = pl.program_id(2)
    lane_q = lax.broadcasted_iota(jnp.int32, (TQ, LANES), 1)
    heads = range(MOBA_GROUP)
    lane_h = lax.broadcasted_iota(jnp.int32, (HALF, LANES), 1)
    blk = lax.broadcasted_iota(jnp.int32, (N_KV_BLOCKS, HALF), 0)
    pad_rows = jnp.zeros((LANES - N_KV_BLOCKS, HALF), F32)
    pair_lanes = lambda hh: slice((hh // 2) * LANES, (hh // 2 + 1) * LANES)
    qm = []
    for hh in heads:
        q2 = q_ref[:, pair_lanes(hh)]
        qm.append(jnp.where((lane_q // HEAD_DIM) == hh % 2, q2, jnp.zeros_like(q2)))
    q_plain = [jnp.concatenate(
        [qm[hh], jnp.broadcast_to(qaug_ref[hh:hh + 1, :], (TQ, LANES)).astype(BF16)], axis=1)
        for hh in heads]

    def select_lanes(t, hh):
        need_rank = 2 * t + 1 > MOBA_TOPK
        if need_rank:
            km = jnp.concatenate(
                [kmean_ref[:, pair_lanes(hh)], jnp.zeros((LANES - N_KV_BLOCKS, LANES), F32)],
                axis=0).astype(BF16)
            gt = _dot_nt(km, qm[hh])
        qx = []
        for half in range(2):
            qb = 2 * t + half
            if qb <= MOBA_TOPK:
                attend = blk <= qb
            else:
                g = gt[:N_KV_BLOCKS, half * HALF:(half + 1) * HALF]
                cnt = jnp.zeros((N_KV_BLOCKS, HALF), F32)
                for mm in range(qb):
                    gm = g[mm:mm + 1, :]
                    beats = (gm > g) | ((gm == g) & (mm < blk))
                    cnt = cnt + jnp.where(beats, 1.0, 0.0)
                attend = ((blk < qb) & (cnt < MOBA_TOPK)) | (blk == qb)
            selb_t = jnp.where(attend, 0.0, NEG)
            selb = jnp.concatenate([selb_t, pad_rows], axis=0).T
            qx.append(jnp.where(lane_h < N_KV_BLOCKS, selb, qaug_ref[hh:hh + 1, :]).astype(BF16))
        return jnp.concatenate(qx, axis=0)

    def finish(t):
        q_list = [functools.partial(
            lambda hh: jnp.concatenate([qm[hh], select_lanes(t, hh)], axis=1), hh)
            for hh in heads]
        acc = _flash_chains(
            q_list,
            lambda c, st, size: k_ref[pl.ds(st, size), (c // 2) * 2 * LANES:(c // 2 + 1) * 2 * LANES],
            lambda c, st, size: vt_ref[c * MOBA_NV:(c + 1) * MOBA_NV, pl.ds(st, size)],
            t, q_first=q_plain)
        for pr in range(MOBA_GROUP // 2):
            out_t = jnp.concatenate(
                [a[:HEAD_DIM] / a[HEAD_DIM:HEAD_DIM + 1] for a in acc[2 * pr:2 * pr + 2]],
                axis=0)
            o_ref[:, pr * LANES:(pr + 1) * LANES] = out_t.T.astype(BF16)

    _per_query_tile(ti, finish)


def _moba_call(qaug, q, k, vt, kmean):
    nq = SEQ // TQ
    g = MOBA_GROUP
    return pl.pallas_call(
        _moba_kernel,
        grid=(BATCH, MOBA_HEADS // g, nq),
        in_specs=[
            pl.BlockSpec((None, 8, LANES), lambda b, h, i: (h, 0, 0)),
            pl.BlockSpec((TQ, g * HEAD_DIM), lambda b, h, i: (b * nq + i, h)),
            pl.BlockSpec((SEQ, g * LANES), lambda b, h, i: (b, h)),
            pl.BlockSpec((None, g * MOBA_NV, SEQ), lambda b, h, i: (b, h, 0)),
            pl.BlockSpec((None, N_KV_BLOCKS, g * HEAD_DIM), lambda b, h, i: (b, 0, h)),
        ],
        out_specs=pl.BlockSpec((TQ, g * HEAD_DIM), lambda b, h, i: (b * nq + i, h)),
        out_shape=jax.ShapeDtypeStruct((N_TOK, MOBA_W), BF16),
        compiler_params=pltpu.CompilerParams(
            dimension_semantics=("arbitrary", "arbitrary", "arbitrary"),
            vmem_limit_bytes=VMEM_LIMIT),
        name="moba",
    )(qaug, q, k, vt, kmean)


MLA_GROUP = 4


MLA_NV = MLA_V + ONES_ROWS


def _mla_kernel(q_ref, k_ref, vt_ref, o_ref):
    ti = pl.program_id(2)
    q_list = [q_ref[:, c * LANES:(c + 1) * LANES] for c in range(MLA_GROUP)]

    def finish(t):
        acc = _flash_chains(
            q_list,
            lambda c, st, size: k_ref[pl.ds(st, size), c * LANES:(c + 1) * LANES],
            lambda c, st, size: vt_ref[c * MLA_NV:(c + 1) * MLA_NV, pl.ds(st, size)],
            t)
        for pr in range(MLA_GROUP // 2):
            out_t = jnp.concatenate(
                [a[:MLA_V] / a[MLA_V:MLA_V + 1] for a in acc[2 * pr:2 * pr + 2]], axis=0)
            o_ref[:, pr * LANES:(pr + 1) * LANES] = out_t.T.astype(BF16)

    _per_query_tile(ti, finish)


def _mla_call(q, k, vt):
    nq = SEQ // TQ
    g = MLA_GROUP
    return pl.pallas_call(
        _mla_kernel,
        grid=(BATCH, MLA_HEADS // g, nq),
        in_specs=[
            pl.BlockSpec((TQ, g * LANES), lambda b, h, i: (b * nq + i, h)),
            pl.BlockSpec((SEQ, g * LANES), lambda b, h, i: (b, h)),
            pl.BlockSpec((None, g * MLA_NV, SEQ), lambda b, h, i: (b, h, 0)),
        ],
        out_specs=pl.BlockSpec((TQ, g * MLA_V), lambda b, h, i: (b * nq + i, h)),
        out_shape=jax.ShapeDtypeStruct((N_TOK, MLA_W), BF16),
        compiler_params=pltpu.CompilerParams(
            dimension_semantics=("arbitrary", "arbitrary", "arbitrary"),
            vmem_limit_bytes=VMEM_LIMIT),
        name="mla",
    )(q, k, vt)


DIFF_GROUP = 4


DIFF_NV = DIFF_V + ONES_ROWS


def _diff_kernel(lam_init, qaug_ref, q_ref, k_ref, vt_ref, lam_ref, g_ref, o_ref):
    ti = pl.program_id(2)
    lane_q = lax.broadcasted_iota(jnp.int32, (TQ, LANES), 1)
    q_list = []
    for hh in range(DIFF_GROUP):
        q2 = q_ref[:, hh * LANES:(hh + 1) * LANES]
        qx = jnp.broadcast_to(qaug_ref[hh:hh + 1, :], (TQ, LANES)).astype(BF16)
        for c in range(2):
            qm = jnp.where((lane_q // DIFF_QK) == c, q2, jnp.zeros_like(q2))
            q_list.append(jnp.concatenate([qm, qx], axis=1))

    lf = lam_ref[...]
    e1 = jnp.exp(jnp.sum(lf[0:1, :] * lf[1:2, :], axis=-1, keepdims=True))
    e2 = jnp.exp(jnp.sum(lf[2:3, :] * lf[3:4, :], axis=-1, keepdims=True))
    lam = e1 - e2 + lam_init

    def finish(t):
        acc = _flash_chains(
            q_list,
            lambda c, st, size: k_ref[pl.ds(st, size),
                                      (c // 2) * 2 * LANES:(c // 2 + 1) * 2 * LANES],
            lambda c, st, size: vt_ref[(c // 2) * DIFF_NV:(c // 2 + 1) * DIFF_NV,
                                       pl.ds(st, size)],
            t)
        for hh in range(DIFF_GROUP):
            a0, a1 = acc[2 * hh], acc[2 * hh + 1]
            o0 = a0[:DIFF_V] / a0[DIFF_V:DIFF_V + 1]
            o1 = a1[:DIFF_V] / a1[DIFF_V:DIFF_V + 1]
            y = (o0 - lam * o1).T
            y = y * lax.rsqrt(jnp.mean(y * y, axis=-1, keepdims=True) + 1e-5) * g_ref[...]
            o_ref[:, hh * LANES:(hh + 1) * LANES] = (y * (1.0 - lam_init)).astype(BF16)

    _per_query_tile(ti, finish)


def _diff_call(lam_init, qaug, q, k, vt, lam_p, subln_g):
    nq = SEQ // TQ
    g = DIFF_GROUP
    return pl.pallas_call(
        functools.partial(_diff_kernel, lam_init),
        grid=(BATCH, DIFF_HEADS // g, nq),
        in_specs=[
            pl.BlockSpec((None, 8, LANES), lambda b, h, i: (h, 0, 0)),
            pl.BlockSpec((TQ, g * LANES), lambda b, h, i: (b * nq + i, h)),
            pl.BlockSpec((SEQ, g * 2 * LANES), lambda b, h, i: (b, h)),
            pl.BlockSpec((None, g * DIFF_NV, SEQ), lambda b, h, i: (b, h, 0)),
            pl.BlockSpec(lam_p.shape, lambda b, h, i: (0, 0)),
            pl.BlockSpec(subln_g.shape, lambda b, h, i: (0, 0)),
        ],
        out_specs=pl.BlockSpec((TQ, g * LANES), lambda b, h, i: (b * nq + i, h)),
        out_shape=jax.ShapeDtypeStruct((N_TOK, DIFF_W), BF16),
        compiler_params=pltpu.CompilerParams(
            dimension_semantics=("arbitrary", "arbitrary", "arbitrary"),
            vmem_limit_bytes=VMEM_LIMIT),
        name="diff",
    )(qaug, q, k, vt, lam_p, subln_g)


def _merge_kernel(x_ref, ya_ref, yb_ref, yc_ref, za_ref, zb_ref, zc_ref, p_ref,
                  wm_ref, bm_ref, wa_ref, wb_ref, wc_ref, wo_ref, wpg_ref, wp_ref,
                  lng_ref, lnb_ref, o_ref):
    subs = [slice(s * TM_SUB, (s + 1) * TM_SUB) for s in range(TM_MERGE // TM_SUB)]
    branches = ((ya_ref, za_ref, wa_ref), (yb_ref, zb_ref, wb_ref), (yc_ref, zc_ref, wc_ref))

    def merge_branches(rows):
        xb = x_ref[rows, :].astype(BF16)
        merged = jnp.zeros((TM_SUB, D_MODEL), F32)
        for i, (y_ref, z_ref, w_ref) in enumerate(branches):
            sl = slice(i * D_MODEL, (i + 1) * D_MODEL)
            gate = jax.nn.sigmoid(_dot(xb, wm_ref[:, sl]) + bm_ref[:, sl])
            z = z_ref[rows, :].astype(F32)
            y = y_ref[rows, :].astype(F32) * (z * jax.nn.sigmoid(z))
            merged = merged + gate * _dot(y.astype(BF16), w_ref[...])
        return merged

    merged = [merge_branches(rows) for rows in subs]
    ple = [_dot(p_ref[rows, :].astype(BF16), wp_ref[...]) for rows in subs]
    r = [ALPHA * x_ref[rows, :] + _dot(m.astype(BF16), wo_ref[...])
         for rows, m in zip(subs, merged)]
    r = [ri + jax.nn.sigmoid(_dot(ri.astype(BF16), wpg_ref[...])) * pi for ri, pi in zip(r, ple)]
    for rows, ri in zip(subs, r):
        mu = jnp.mean(ri, axis=-1, keepdims=True)
        d = ri - mu
        var = jnp.mean(d * d, axis=-1, keepdims=True)
        o_ref[rows, :] = d * lax.rsqrt(var + NORM_EPS) * lng_ref[...] + lnb_ref[...]


def _merge_call(x2, ya, yb, yc, za, zb, zc, p2, wm, bm, wa, wb, wc, wo, wpg, wp, lng, lnb):
    row = lambda w: pl.BlockSpec((TM_MERGE, w), lambda i: (i, 0))
    consts = (wm, bm, wa, wb, wc, wo, wpg, wp, lng, lnb)
    return pl.pallas_call(
        _merge_kernel,
        grid=(N_TOK // TM_MERGE,),
        in_specs=[row(D_MODEL)] + [row(MOBA_W)] * 6 + [row(PLE_DIM)]
                 + [_const_spec(c.shape) for c in consts],
        out_specs=row(D_MODEL),
        out_shape=jax.ShapeDtypeStruct((N_TOK, D_MODEL), F32),
        compiler_params=pltpu.CompilerParams(
            dimension_semantics=("arbitrary",), vmem_limit_bytes=VMEM_LIMIT),
        name="merge",
    )(x2, ya, yb, yc, za, zb, zc, p2, *consts)


def _prep_proj_weights(w_in, w_uq, w_ukv):
    pts = np.cumsum([MOBA_W] * 4 + [MLA_Q_LORA, MLA_KV_LORA, MLA_ROPE, MLA_W] + [DIFF_W] * 4)
    a, cq, ckv, kr, bz, c = (w_in[:, :pts[3]], w_in[:, pts[3]:pts[4]], w_in[:, pts[4]:pts[5]],
                             w_in[:, pts[5]:pts[6]], w_in[:, pts[6]:pts[7]], w_in[:, pts[7]:])
    z_lo = jnp.zeros((D_MODEL, MLA_NOPE), F32)
    z_hi = jnp.zeros((D_MODEL, LANES - MLA_NOPE - MLA_ROPE), F32)
    kr128 = jnp.concatenate([z_lo, kr, z_hi], axis=-1)
    krs128 = jnp.concatenate([z_lo, _rot_half_cols(kr), z_hi], axis=-1)
    wcat = jnp.concatenate([a, cq, ckv, kr128, krs128, bz, c], axis=-1).astype(BF16)

    dq = MLA_NOPE + MLA_ROPE
    uq = w_uq.reshape(MLA_Q_LORA, MLA_HEADS, dq)
    pad_q = jnp.zeros((MLA_Q_LORA, MLA_HEADS, LANES - dq), F32)
    wq = jnp.concatenate([uq, pad_q], axis=-1)
    wqs = jnp.concatenate([jnp.zeros_like(uq[..., :MLA_NOPE]),
                           _rot_half_cols(uq[..., MLA_NOPE:]), pad_q], axis=-1)
    ukv = w_ukv.reshape(MLA_KV_LORA, MLA_HEADS, MLA_NOPE + MLA_V)
    wkn = jnp.concatenate([ukv[..., :MLA_NOPE],
                           jnp.zeros((MLA_KV_LORA, MLA_HEADS, LANES - MLA_NOPE), F32)], axis=-1)
    wv = ukv[..., MLA_NOPE:]
    flat = lambda w: w.reshape(w.shape[0], -1).astype(BF16)
    return wcat, flat(wq), flat(wqs), flat(wkn), flat(wv)


def kernel(x, p, w_in, mla_q_norm_g, mla_kv_norm_g, mla_w_uq, mla_w_ukv, diff_lambda, diff_subln_g, w_branch_a, w_branch_b, w_branch_c, w_merge, b_merge, w_out, ln_g, ln_b, w_ple_gate, w_ple):
    slopes_a, slopes_c = _alibi_slopes()
    qaug_a = _query_aug_rows(slopes_a, MOBA_GROUP)
    qaug_c = _query_aug_rows(slopes_c, DIFF_GROUP)
    ktab = _key_aug_table()
    cos_t, sin_t = _rope_tables()
    h = x.reshape(N_TOK, D_MODEL)
    for i in range(DEPTH):
        wcat, wq, wqs, wkn, wv = _prep_proj_weights(w_in[i], mla_w_uq[i], mla_w_ukv[i])
        (aq, ak, avt, az, kmean, bq, bk, bvt, bz, cq, ck, cvt, cz) = _proj_call(
            h, wcat, wq, wqs, wkn, wv, mla_q_norm_g[i][None, :], mla_kv_norm_g[i][None, :],
            cos_t, sin_t, ktab)
        kmean = kmean.reshape(BATCH, N_KV_BLOCKS, MOBA_W)
        ya = _moba_call(qaug_a, aq, ak, avt, kmean)
        yb = _mla_call(bq, bk, bvt)
        lam_init = 0.8 - 0.6 * math.exp(-0.3 * i)
        yc = _diff_call(lam_init, qaug_c, cq, ck, cvt, diff_lambda[i], diff_subln_g[i][None, :])
        bf = lambda w: w.astype(BF16)
        h = _merge_call(h, ya, yb, yc, az, bz, cz, p[i].reshape(N_TOK, PLE_DIM),
                        bf(w_merge[i]), b_merge[i][None, :], bf(w_branch_a[i]),
                        bf(w_branch_b[i]), bf(w_branch_c[i]), bf(w_out[i]), bf(w_ple_gate[i]),
                        bf(w_ple[i]), ln_g[i][None, :], ln_b[i][None, :])
    return h.reshape(BATCH, SEQ, D_MODEL)
```

```python
import functools
import math

import numpy as np
import jax
import jax.numpy as jnp
from jax import lax
from jax.experimental import pallas as pl
from jax.experimental.pallas import tpu as pltpu

D_MODEL = 1024
BATCH = 8
SEQ = 2048
DEPTH = 2
MOBA_HEADS = 8
HEAD_DIM = 64
MOBA_BLOCK = 256
MOBA_TOPK = 3
MLA_HEADS = 8
MLA_Q_LORA = 384
MLA_KV_LORA = 256
MLA_NOPE = 64
MLA_ROPE = 32
MLA_V = 64
ROPE_THETA = 10000.0
DIFF_HEADS = 4
DIFF_QK = 64
DIFF_V = 2 * DIFF_QK
PLE_DIM = 256
NORM_EPS = 1e-5
NEG = -1e30
N_BRANCH = 3
ALPHA = (2 * DEPTH) ** 0.25
LOG2E = math.log2(math.e)

MOBA_W = MOBA_HEADS * HEAD_DIM
MLA_W = MLA_HEADS * MLA_V
DIFF_W = DIFF_HEADS * DIFF_V
N_TOK = BATCH * SEQ
N_KV_BLOCKS = SEQ // MOBA_BLOCK

LANES = 128
VMEM_LIMIT = 54 * 1024 * 1024
TM = 256
TM_MERGE = 512
TM_SUB = 256
TQ = 512
TK = 512
HALF = TQ // 2

OFF_A = 0
OFF_CQ = OFF_A + 4 * MOBA_W
OFF_CKV = OFF_CQ + MLA_Q_LORA
OFF_KR = OFF_CKV + MLA_KV_LORA
OFF_BZ = OFF_KR + LANES
OFF_C = OFF_BZ + MLA_W
W_CAT = OFF_C + 4 * DIFF_W

F32 = jnp.float32
BF16 = jnp.bfloat16


def _dot(a, b):
    return jnp.dot(a, b, preferred_element_type=F32)


def _dot_nt(a, b):
    return lax.dot_general(a, b, (((1,), (1,)), ((), ())), preferred_element_type=F32)


def _alibi_slopes():
    n = MOBA_HEADS + DIFF_HEADS
    s = 2.0 ** (-8.0 * (np.arange(n) + 1) / n)
    diff_idx = np.arange(DIFF_HEADS) * (n // DIFF_HEADS)
    moba_idx = np.setdiff1d(np.arange(n), diff_idx)
    return (jnp.asarray(s[moba_idx], dtype=F32), jnp.asarray(s[diff_idx], dtype=F32))


def _rope_tables():
    d = MLA_ROPE
    freqs = ROPE_THETA ** (-np.arange(0, d, 2, dtype=np.float32) / d)
    ang = np.arange(SEQ, dtype=np.float32)[:, None] * freqs[None, :]
    cos, sin = np.cos(ang), np.sin(ang)
    cos_t = np.zeros((SEQ, LANES), np.float32)
    sin_t = np.zeros((SEQ, LANES), np.float32)
    cos_t[:, :MLA_NOPE] = 1.0
    cos_t[:, MLA_NOPE:MLA_NOPE + d] = np.concatenate([cos, cos], axis=-1)
    sin_t[:, MLA_NOPE:MLA_NOPE + d] = np.concatenate([sin, sin], axis=-1)
    return jnp.asarray(cos_t), jnp.asarray(sin_t)


def _rot_half_cols(w):
    half = MLA_ROPE // 2
    return jnp.concatenate([-w[..., half:], w[..., :half]], axis=-1)


def _proj_kernel(x_ref, w_ref, wq_ref, wqs_ref, wkv_ref, gq_ref, gkv_ref,
                 cos_ref, sin_ref, ktab_ref,
                 aq_ref, ak_ref, avt_ref, az_ref, kmean_ref,
                 bq_ref, bk_ref, bvt_ref, bz_ref,
                 cq_ref, ck_ref, cvt_ref, cz_ref):
    xb = x_ref[...].astype(BF16)
    ktab = ktab_ref[...]
    ones_rows = jnp.where(lax.broadcasted_iota(jnp.int32, (ONES_ROWS, TM), 0) == 0,
                          1.0, 0.0).astype(BF16)

    def store_vt(vt_ref, v, dv, stride=None, offset=0):
        stride = stride or dv
        vt = v.T.astype(BF16)
        for h in range(v.shape[1] // stride):
            base = h * (dv + ONES_ROWS)
            vt_ref[base:base + dv, :] = vt[h * stride + offset:h * stride + offset + dv]
            vt_ref[base + dv:base + dv + ONES_ROWS, :] = ones_rows

    def store_k_aug(k_ref, k):
        for h in range(k.shape[1] // LANES):
            k_ref[:, 2 * h * LANES:(2 * h + 1) * LANES] = (
                k[:, h * LANES:(h + 1) * LANES].astype(BF16))
            k_ref[:, (2 * h + 1) * LANES:(2 * h + 2) * LANES] = ktab

    def proj(off, width):
        return _dot(xb, w_ref[:, off:off + width])

    cos_t = cos_ref[...]
    sin_t = sin_ref[...]
    cq = proj(OFF_CQ, MLA_Q_LORA)
    ckv = proj(OFF_CKV, MLA_KV_LORA)
    kr = proj(OFF_KR, LANES)
    krot = kr * cos_t + pltpu.roll(kr, LANES - MLA_ROPE, 1) * sin_t
    cqn = cq * lax.rsqrt(jnp.mean(cq * cq, axis=-1, keepdims=True) + 1e-6) * gq_ref[...]
    cqn = cqn.astype(BF16)
    ckvn = ckv * lax.rsqrt(jnp.mean(ckv * ckv, axis=-1, keepdims=True) + 1e-6) * gkv_ref[...]
    ckvn = ckvn.astype(BF16)

    aq_ref[...] = (proj(OFF_A, MOBA_W) * (HEAD_DIM ** -0.5 * LOG2E)).astype(BF16)
    ka = proj(OFF_A + MOBA_W, MOBA_W)
    store_k_aug(ak_ref, ka)
    kmean_ref[0] = jnp.mean(ka, axis=0, keepdims=True)
    store_vt(avt_ref, proj(OFF_A + 2 * MOBA_W, MOBA_W), HEAD_DIM)
    az_ref[...] = proj(OFF_A + 3 * MOBA_W, MOBA_W).astype(BF16)

    qa = _dot(cqn, wq_ref[...])
    qb = _dot(cqn, wqs_ref[...])
    qscale = (MLA_NOPE + MLA_ROPE) ** -0.5 * LOG2E
    cos_q = cos_t * qscale
    sin_q = sin_t * qscale
    heads_per_tile = LANES // MLA_ROPE
    for h in range(MLA_HEADS):
        sl = slice(h * LANES, (h + 1) * LANES)
        qb_tile = qb[:, (h // heads_per_tile) * LANES:(h // heads_per_tile + 1) * LANES]
        shift = (MLA_NOPE - (h % heads_per_tile) * MLA_ROPE) % LANES
        partner = pltpu.roll(qb_tile, shift, 1) if shift else qb_tile
        bq_ref[:, sl] = (qa[:, sl] * cos_q + partner * sin_q).astype(BF16)
    kv = _dot(ckvn, wkv_ref[...])
    nope_lanes = lax.broadcasted_iota(jnp.int32, (TM, LANES), 1) < MLA_NOPE
    for h in range(MLA_HEADS):
        sl = slice(h * LANES, (h + 1) * LANES)
        bk_ref[:, sl] = jnp.where(nope_lanes, kv[:, sl], krot).astype(BF16)
    store_vt(bvt_ref, kv, MLA_V, stride=MLA_NOPE + MLA_V, offset=MLA_NOPE)
    bz_ref[...] = proj(OFF_BZ, MLA_W).astype(BF16)

    cq_ref[...] = (proj(OFF_C, DIFF_W) * (DIFF_QK ** -0.5 * LOG2E)).astype(BF16)
    store_k_aug(ck_ref, proj(OFF_C + DIFF_W, DIFF_W))
    store_vt(cvt_ref, proj(OFF_C + 2 * DIFF_W, DIFF_W), DIFF_V)
    cz_ref[...] = proj(OFF_C + 3 * DIFF_W, DIFF_W).astype(BF16)


def _layer_spec(stacked, li):
    zeros = (0,) * (stacked.ndim - 1)
    return pl.BlockSpec((None,) + stacked.shape[1:], lambda *_: (li,) + zeros,
                        pipeline_mode=pl.Buffered(1))


def _proj_call(li, x2, wcat, wq, wqs, wkv, gq, gkv, cos_t, sin_t, ktab):
    n_tiles = N_TOK // TM
    pos_tiles = SEQ // TM
    row = lambda w: pl.BlockSpec((TM, w), lambda i: (i, 0))
    tab = pl.BlockSpec((TM, LANES), lambda i: (i % pos_tiles, 0))
    out_shapes = []
    out_specs = []

    def add(width):
        out_shapes.append(jax.ShapeDtypeStruct((N_TOK, width), BF16))
        out_specs.append(row(width))

    def add_vt(heads, dv):
        rows = heads * (dv + ONES_ROWS)
        out_shapes.append(jax.ShapeDtypeStruct((BATCH, rows, SEQ), BF16))
        out_specs.append(pl.BlockSpec((None, rows, TM),
                                      lambda i: (i // pos_tiles, 0, i % pos_tiles)))

    add(MOBA_W); add(2 * MOBA_W); add_vt(MOBA_HEADS, HEAD_DIM); add(MOBA_W)
    out_shapes.append(jax.ShapeDtypeStruct((n_tiles, 1, MOBA_W), F32))
    out_specs.append(pl.BlockSpec((1, 1, MOBA_W), lambda i: (i, 0, 0)))
    add(MLA_HEADS * LANES); add(MLA_HEADS * LANES); add_vt(MLA_HEADS, MLA_V); add(MLA_W)
    add(DIFF_W); add(2 * DIFF_W); add_vt(DIFF_HEADS, DIFF_V); add(DIFF_W)
    return pl.pallas_call(
        _proj_kernel,
        grid=(n_tiles,),
        in_specs=[row(D_MODEL)] + [_layer_spec(w, li) for w in (wcat, wq, wqs, wkv, gq, gkv)]
                 + [tab, tab, tab],
        out_specs=out_specs,
        out_shape=out_shapes,
        compiler_params=pltpu.CompilerParams(
            dimension_semantics=("arbitrary",), vmem_limit_bytes=VMEM_LIMIT),
        name="proj",
    )(x2, wcat, wq, wqs, wkv, gq, gkv, cos_t, sin_t, ktab)


AUG_POS = 8
ONES_ROWS = 16


def _split3(v):
    hi = v.astype(BF16).astype(F32)
    mid = (v - hi).astype(BF16).astype(F32)
    lo = (v - hi - mid).astype(BF16).astype(F32)
    return hi, mid, lo


def _key_aug_table():
    kpos = np.arange(SEQ)
    t = np.zeros((SEQ, LANES), np.float32)
    t[kpos, kpos // MOBA_BLOCK] = 1.0
    t[:, AUG_POS:AUG_POS + 3] = (kpos // 256)[:, None]
    t[:, AUG_POS + 3:AUG_POS + 6] = (kpos % 256)[:, None]
    return jnp.asarray(t).astype(BF16)


def _query_aug_rows(slopes, group):
    s2 = slopes * LOG2E
    pieces = jnp.stack(_split3(s2 * 256.0) + _split3(s2), axis=-1)
    rows = jnp.zeros((slopes.shape[0], LANES), F32).at[:, AUG_POS:AUG_POS + 6].set(pieces)
    rows = rows.reshape(slopes.shape[0] // group, group, LANES)
    return jnp.pad(rows, ((0, 0), (0, 8 - group), (0, 0)))


def _flash_chains(q_list, k_tile, vt_tile, ti, q_first=None):
    n = len(q_list)
    key_i = lax.broadcasted_iota(jnp.int32, (TK, HALF), 0)
    qry_i = lax.broadcasted_iota(jnp.int32, (TK, HALF), 1)
    tri_lo = (key_i <= qry_i)[:HALF]
    tri_hi = key_i <= qry_i + HALF

    q_cache = {}

    def q_sub(sc):
        if sc // 2 not in q_cache:
            q = q_list[sc // 2]
            q_cache[sc // 2] = q() if callable(q) else q
        return q_cache[sc // 2][(sc % 2) * HALF:(sc % 2 + 1) * HALF]

    d0 = ti * TK
    items = []
    for sc in range(2 * n):
        items.append((sc, d0, TK, tri_hi) if sc % 2 else (sc, d0, HALF, tri_lo))
    for j in range(ti):
        items += [(sc, j * TK, TK, None) for sc in range(2 * n)]

    m = [None] * (2 * n)
    acc = [None] * (2 * n)
    live = {}

    def scores(i):
        sc, start, size, mask = items[i]
        if q_first is not None and m[sc] is None and sc % 2 == 0:
            q = q_first[sc // 2][:HALF]
        else:
            q = q_sub(sc)
        s = _dot_nt(k_tile(sc // 2, start, size), q)
        if mask is not None:
            s = jnp.where(mask, s, NEG)
        smax = jnp.max(s, axis=0, keepdims=True)
        m_new = smax if m[sc] is None else jnp.maximum(m[sc], smax)
        a = None if m[sc] is None else jnp.exp2(m[sc] - m_new)
        m[sc] = m_new
        live[i] = (s, m_new, a)

    def probs(i):
        s, m_new, a = live[i]
        live[i] = (jnp.exp2(s - m_new).astype(BF16), a)

    def values(i):
        sc, start, size, _ = items[i]
        p, a = live.pop(i)
        pv = _dot(vt_tile(sc // 2, start, size), p)
        acc[sc] = pv if a is None else a * acc[sc] + pv

    for r in range(0, len(items), 2 * n):
        rnd = range(r, r + 2 * n)
        for i in rnd:
            scores(i)
        for i in rnd:
            probs(i)
        for i in rnd:
            values(i)
    return [jnp.concatenate([acc[2 * c], acc[2 * c + 1]], axis=1) for c in range(n)]


def _per_query_tile(ti, fn):
    for t in range(SEQ // TQ):
        pl.when(ti == t)(functools.partial(fn, t))


MOBA_NV = HEAD_DIM + ONES_ROWS
MOBA_GROUP = 4


def _moba_kernel(qaug_ref, q_ref, k_ref, vt_ref, kmean_ref, o_ref):
    ti = pl.program_id(2)
    lane_q = lax.broadcasted_iota(jnp.int32, (TQ, LANES), 1)
    heads = range(MOBA_GROUP)
    lane_h = lax.broadcasted_iota(jnp.int32, (HALF, LANES), 1)
    blk = lax.broadcasted_iota(jnp.int32, (N_KV_BLOCKS, HALF), 0)
    pad_rows = jnp.zeros((LANES - N_KV_BLOCKS, HALF), F32)
    pair_lanes = lambda hh: slice((hh // 2) * LANES, (hh // 2 + 1) * LANES)
    qm = []
    for hh in heads:
        q2 = q_ref[:, pair_lanes(hh)]
        qm.append(jnp.where((lane_q // HEAD_DIM) == hh % 2, q2, jnp.zeros_like(q2)))
    q_plain = [jnp.concatenate(
        [qm[hh], jnp.broadcast_to(qaug_ref[hh:hh + 1, :], (TQ, LANES)).astype(BF16)], axis=1)
        for hh in heads]

    def select_lanes(t, hh):
        need_rank = 2 * t + 1 > MOBA_TOPK
        if need_rank:
            km = jnp.concatenate(
                [kmean_ref[:, pair_lanes(hh)], jnp.zeros((LANES - N_KV_BLOCKS, LANES), F32)],
                axis=0).astype(BF16)
            gt = _dot_nt(km, qm[hh])
        qx = []
        for half in range(2):
            qb = 2 * t + half
            if qb <= MOBA_TOPK:
                attend = blk <= qb
            else:
                g = gt[:N_KV_BLOCKS, half * HALF:(half + 1) * HALF]
                cnt = jnp.zeros((N_KV_BLOCKS, HALF), F32)
                for mm in range(qb):
                    gm = g[mm:mm + 1, :]
                    beats = (gm > g) | ((gm == g) & (mm < blk))
                    cnt = cnt + jnp.where(beats, 1.0, 0.0)
                attend = ((blk < qb) & (cnt < MOBA_TOPK)) | (blk == qb)
            selb_t = jnp.where(attend, 0.0, NEG)
            selb = jnp.concatenate([selb_t, pad_rows], axis=0).T
            qx.append(jnp.where(lane_h < N_KV_BLOCKS, selb, qaug_ref[hh:hh + 1, :]).astype(BF16))
        return jnp.concatenate(qx, axis=0)

    def finish(t):
        q_list = [functools.partial(
            lambda hh: jnp.concatenate([qm[hh], select_lanes(t, hh)], axis=1), hh)
            for hh in heads]
        acc = _flash_chains(
            q_list,
            lambda c, st, size: k_ref[pl.ds(st, size), (c // 2) * 2 * LANES:(c // 2 + 1) * 2 * LANES],
            lambda c, st, size: vt_ref[c * MOBA_NV:(c + 1) * MOBA_NV, pl.ds(st, size)],
            t, q_first=q_plain)
        for pr in range(MOBA_GROUP // 2):
            out_t = jnp.concatenate(
                [a[:HEAD_DIM] / a[HEAD_DIM:HEAD_DIM + 1] for a in acc[2 * pr:2 * pr + 2]],
                axis=0)
            o_ref[:, pr * LANES:(pr + 1) * LANES] = out_t.T.astype(BF16)

    _per_query_tile(ti, finish)


def _moba_call(qaug, q, k, vt, kmean):
    nq = SEQ // TQ
    g = MOBA_GROUP
    return pl.pallas_call(
        _moba_kernel,
        grid=(BATCH, MOBA_HEADS // g, nq),
        in_specs=[
            pl.BlockSpec((None, 8, LANES), lambda b, h, i: (h, 0, 0)),
            pl.BlockSpec((TQ, g * HEAD_DIM), lambda b, h, i: (b * nq + i, h)),
            pl.BlockSpec((SEQ, g * LANES), lambda b, h, i: (b, h)),
            pl.BlockSpec((None, g * MOBA_NV, SEQ), lambda b, h, i: (b, h, 0)),
            pl.BlockSpec((None, N_KV_BLOCKS, g * HEAD_DIM), lambda b, h, i: (b, 0, h)),
        ],
        out_specs=pl.BlockSpec((TQ, g * HEAD_DIM), lambda b, h, i: (b * nq + i, h)),
        out_shape=jax.ShapeDtypeStruct((N_TOK, MOBA_W), BF16),
        compiler_params=pltpu.CompilerParams(
            dimension_semantics=("arbitrary", "arbitrary", "arbitrary"),
            vmem_limit_bytes=VMEM_LIMIT),
        name="moba",
    )(qaug, q, k, vt, kmean)


MLA_GROUP = 4


MLA_NV = MLA_V + ONES_ROWS


def _mla_kernel(q_ref, k_ref, vt_ref, o_ref):
    ti = pl.program_id(2)
    q_list = [q_ref[:, c * LANES:(c + 1) * LANES] for c in range(MLA_GROUP)]

    def finish(t):
        acc = _flash_chains(
            q_list,
            lambda c, st, size: k_ref[pl.ds(st, size), c * LANES:(c + 1) * LANES],
            lambda c, st, size: vt_ref[c * MLA_NV:(c + 1) * MLA_NV, pl.ds(st, size)],
            t)
        for pr in range(MLA_GROUP // 2):
            out_t = jnp.concatenate(
                [a[:MLA_V] / a[MLA_V:MLA_V + 1] for a in acc[2 * pr:2 * pr + 2]], axis=0)
            o_ref[:, pr * LANES:(pr + 1) * LANES] = out_t.T.astype(BF16)

    _per_query_tile(ti, finish)


def _mla_call(q, k, vt):
    nq = SEQ // TQ
    g = MLA_GROUP
    return pl.pallas_call(
        _mla_kernel,
        grid=(BATCH, MLA_HEADS // g, nq),
        in_specs=[
            pl.BlockSpec((TQ, g * LANES), lambda b, h, i: (b * nq + i, h)),
            pl.BlockSpec((SEQ, g * LANES), lambda b, h, i: (b, h)),
            pl.BlockSpec((None, g * MLA_NV, SEQ), lambda b, h, i: (b, h, 0)),
        ],
        out_specs=pl.BlockSpec((TQ, g * MLA_V), lambda b, h, i: (b * nq + i, h)),
        out_shape=jax.ShapeDtypeStruct((N_TOK, MLA_W), BF16),
        compiler_params=pltpu.CompilerParams(
            dimension_semantics=("arbitrary", "arbitrary", "arbitrary"),
            vmem_limit_bytes=VMEM_LIMIT),
        name="mla",
    )(q, k, vt)


DIFF_GROUP = 4


DIFF_NV = DIFF_V + ONES_ROWS


def _diff_kernel(lam_init, qaug_ref, q_ref, k_ref, vt_ref, lam_ref, g_ref, o_ref):
    ti = pl.program_id(2)
    lane_q = lax.broadcasted_iota(jnp.int32, (TQ, LANES), 1)
    q_list = []
    for hh in range(DIFF_GROUP):
        q2 = q_ref[:, hh * LANES:(hh + 1) * LANES]
        qx = jnp.broadcast_to(qaug_ref[hh:hh + 1, :], (TQ, LANES)).astype(BF16)
        for c in range(2):
            qm = jnp.where((lane_q // DIFF_QK) == c, q2, jnp.zeros_like(q2))
            q_list.append(jnp.concatenate([qm, qx], axis=1))

    lf = lam_ref[...]
    e1 = jnp.exp(jnp.sum(lf[0:1, :] * lf[1:2, :], axis=-1, keepdims=True))
    e2 = jnp.exp(jnp.sum(lf[2:3, :] * lf[3:4, :], axis=-1, keepdims=True))
    lam = e1 - e2 + lam_init

    def finish(t):
        acc = _flash_chains(
            q_list,
            lambda c, st, size: k_ref[pl.ds(st, size),
                                      (c // 2) * 2 * LANES:(c // 2 + 1) * 2 * LANES],
            lambda c, st, size: vt_ref[(c // 2) * DIFF_NV:(c // 2 + 1) * DIFF_NV,
                                       pl.ds(st, size)],
            t)
        for hh in range(DIFF_GROUP):
            a0, a1 = acc[2 * hh], acc[2 * hh + 1]
            o0 = a0[:DIFF_V] / a0[DIFF_V:DIFF_V + 1]
            o1 = a1[:DIFF_V] / a1[DIFF_V:DIFF_V + 1]
            y = (o0 - lam * o1).T
            y = y * lax.rsqrt(jnp.mean(y * y, axis=-1, keepdims=True) + 1e-5) * g_ref[...]
            o_ref[:, hh * LANES:(hh + 1) * LANES] = (y * (1.0 - lam_init)).astype(BF16)

    _per_query_tile(ti, finish)


def _diff_call(li, lam_init, qaug, q, k, vt, lam_p, subln_g):
    nq = SEQ // TQ
    g = DIFF_GROUP
    return pl.pallas_call(
        functools.partial(_diff_kernel, lam_init),
        grid=(BATCH, DIFF_HEADS // g, nq),
        in_specs=[
            pl.BlockSpec((None, 8, LANES), lambda b, h, i: (h, 0, 0)),
            pl.BlockSpec((TQ, g * LANES), lambda b, h, i: (b * nq + i, h)),
            pl.BlockSpec((SEQ, g * 2 * LANES), lambda b, h, i: (b, h)),
            pl.BlockSpec((None, g * DIFF_NV, SEQ), lambda b, h, i: (b, h, 0)),
            _layer_spec(lam_p, li),
            _layer_spec(subln_g, li),
        ],
        out_specs=pl.BlockSpec((TQ, g * LANES), lambda b, h, i: (b * nq + i, h)),
        out_shape=jax.ShapeDtypeStruct((N_TOK, DIFF_W), BF16),
        compiler_params=pltpu.CompilerParams(
            dimension_semantics=("arbitrary", "arbitrary", "arbitrary"),
            vmem_limit_bytes=VMEM_LIMIT),
        name="diff",
    )(qaug, q, k, vt, lam_p, subln_g)


def _merge_kernel(x_ref, ya_ref, yb_ref, yc_ref, za_ref, zb_ref, zc_ref, p_ref,
                  wm_ref, bm_ref, wa_ref, wb_ref, wc_ref, wo_ref, wpg_ref, wp_ref,
                  lng_ref, lnb_ref, o_ref):
    subs = [slice(s * TM_SUB, (s + 1) * TM_SUB) for s in range(TM_MERGE // TM_SUB)]
    branches = ((ya_ref, za_ref, wa_ref), (yb_ref, zb_ref, wb_ref), (yc_ref, zc_ref, wc_ref))

    def merge_branches(rows):
        xb = x_ref[rows, :].astype(BF16)
        merged = jnp.zeros((TM_SUB, D_MODEL), F32)
        for i, (y_ref, z_ref, w_ref) in enumerate(branches):
            sl = slice(i * D_MODEL, (i + 1) * D_MODEL)
            gate = jax.nn.sigmoid(_dot(xb, wm_ref[:, sl]) + bm_ref[:, sl])
            z = z_ref[rows, :].astype(F32)
            y = y_ref[rows, :].astype(F32) * (z * jax.nn.sigmoid(z))
            merged = merged + gate * _dot(y.astype(BF16), w_ref[...])
        return merged

    r = []
    for rows in subs:
        merged = merge_branches(rows)
        ple = _dot(p_ref[rows, :].astype(BF16), wp_ref[...])
        ri = ALPHA * x_ref[rows, :] + _dot(merged.astype(BF16), wo_ref[...])
        r.append(ri + jax.nn.sigmoid(_dot(ri.astype(BF16), wpg_ref[...])) * ple)
    for rows, ri in zip(subs, r):
        mu = jnp.mean(ri, axis=-1, keepdims=True)
        d = ri - mu
        var = jnp.mean(d * d, axis=-1, keepdims=True)
        o_ref[rows, :] = d * lax.rsqrt(var + NORM_EPS) * lng_ref[...] + lnb_ref[...]


def _merge_call(li, x2, ya, yb, yc, za, zb, zc, p2, wm, bm, wa, wb, wc, wo, wpg, wp, lng, lnb):
    n_tiles = N_TOK // TM_MERGE
    row = lambda w: pl.BlockSpec((TM_MERGE, w), lambda i: (i, 0))
    p_row = pl.BlockSpec((TM_MERGE, PLE_DIM), lambda i: (li * n_tiles + i, 0))
    consts = (wm, bm, wa, wb, wc, wo, wpg, wp, lng, lnb)
    return pl.pallas_call(
        _merge_kernel,
        grid=(n_tiles,),
        in_specs=[row(D_MODEL)] + [row(MOBA_W)] * 6 + [p_row]
                 + [_layer_spec(c, li) for c in consts],
        out_specs=row(D_MODEL),
        out_shape=jax.ShapeDtypeStruct((N_TOK, D_MODEL), F32),
        compiler_params=pltpu.CompilerParams(
            dimension_semantics=("arbitrary",), vmem_limit_bytes=VMEM_LIMIT),
        name="merge",
    )(x2, ya, yb, yc, za, zb, zc, p2, *consts)


def _prep_proj_weights(w_in, w_uq):
    pts = np.cumsum([MOBA_W] * 4 + [MLA_Q_LORA, MLA_KV_LORA, MLA_ROPE, MLA_W] + [DIFF_W] * 4)
    head, kr, tail = w_in[..., :pts[5]], w_in[..., pts[5]:pts[6]], w_in[..., pts[6]:]
    z_lo = jnp.zeros(w_in.shape[:-1] + (MLA_NOPE,), F32)
    wcat = jnp.concatenate([head, z_lo, kr, _rot_half_cols(kr), tail], axis=-1).astype(BF16)

    dq = MLA_NOPE + MLA_ROPE
    uq = w_uq.reshape(DEPTH, MLA_Q_LORA, MLA_HEADS, dq)
    pad_q = jnp.zeros((DEPTH, MLA_Q_LORA, MLA_HEADS, LANES - dq), F32)
    wq = jnp.concatenate([uq, pad_q], axis=-1)
    wqs = _rot_half_cols(uq[..., MLA_NOPE:])
    flat = lambda w: w.reshape(DEPTH, MLA_Q_LORA, -1).astype(BF16)
    return wcat, flat(wq), flat(wqs)


def kernel(x, p, w_in, mla_q_norm_g, mla_kv_norm_g, mla_w_uq, mla_w_ukv, diff_lambda, diff_subln_g, w_branch_a, w_branch_b, w_branch_c, w_merge, b_merge, w_out, ln_g, ln_b, w_ple_gate, w_ple):
    slopes_a, slopes_c = _alibi_slopes()
    qaug_a = _query_aug_rows(slopes_a, MOBA_GROUP)
    qaug_c = _query_aug_rows(slopes_c, DIFF_GROUP)
    ktab = _key_aug_table()
    cos_t, sin_t = _rope_tables()
    bf = lambda w: w.astype(BF16)
    vec = lambda v: v[:, None, :]
    wcat, wq, wqs = _prep_proj_weights(w_in, mla_w_uq)
    wkv = bf(mla_w_ukv)
    merge_consts = (bf(w_merge), vec(b_merge), bf(w_branch_a), bf(w_branch_b), bf(w_branch_c),
                    bf(w_out), bf(w_ple_gate), bf(w_ple), vec(ln_g), vec(ln_b))
    gq, gkv, subln_g = vec(mla_q_norm_g), vec(mla_kv_norm_g), vec(diff_subln_g)
    p2 = p.reshape(DEPTH * N_TOK, PLE_DIM)
    h = x.reshape(N_TOK, D_MODEL)
    for i in range(DEPTH):
        (aq, ak, avt, az, kmean, bq, bk, bvt, bz, cq, ck, cvt, cz) = _proj_call(
            i, h, wcat, wq, wqs, wkv, gq, gkv, cos_t, sin_t, ktab)
        kmean = kmean.reshape(BATCH, N_KV_BLOCKS, MOBA_W)
        ya = _moba_call(qaug_a, aq, ak, avt, kmean)
        yb = _mla_call(bq, bk, bvt)
        lam_init = 0.8 - 0.6 * math.exp(-0.3 * i)
        yc = _diff_call(i, lam_init, qaug_c, cq, ck, cvt, diff_lambda, subln_g)
        h = _merge_call(i, h, ya, yb, yc, az, bz, cz, p2, *merge_consts)
    return h.reshape(BATCH, SEQ, D_MODEL)
```

```python
import functools
import math

import numpy as np
import jax
import jax.numpy as jnp
from jax import lax
from jax.experimental import pallas as pl
from jax.experimental.pallas import tpu as pltpu

D_MODEL = 1024
BATCH = 8
SEQ = 2048
DEPTH = 2
MOBA_HEADS = 8
HEAD_DIM = 64
MOBA_BLOCK = 256
MOBA_TOPK = 3
MLA_HEADS = 8
MLA_Q_LORA = 384
MLA_KV_LORA = 256
MLA_NOPE = 64
MLA_ROPE = 32
MLA_V = 64
ROPE_THETA = 10000.0
DIFF_HEADS = 4
DIFF_QK = 64
DIFF_V = 2 * DIFF_QK
PLE_DIM = 256
NORM_EPS = 1e-5
NEG = -1e30
N_BRANCH = 3
ALPHA = (2 * DEPTH) ** 0.25
LOG2E = math.log2(math.e)

MOBA_W = MOBA_HEADS * HEAD_DIM
MLA_W = MLA_HEADS * MLA_V
DIFF_W = DIFF_HEADS * DIFF_V
N_TOK = BATCH * SEQ
N_KV_BLOCKS = SEQ // MOBA_BLOCK

LANES = 128
VMEM_LIMIT = 54 * 1024 * 1024
TM = 256
TM_MERGE = 512
TM_SUB = 256
TQ = 1024
TK = 512
QSUB = MOBA_BLOCK
NSUB = TQ // QSUB

OFF_A = 0
OFF_CQ = OFF_A + 4 * MOBA_W
OFF_CKV = OFF_CQ + MLA_Q_LORA
OFF_KR = OFF_CKV + MLA_KV_LORA
OFF_BZ = OFF_KR + LANES
OFF_C = OFF_BZ + MLA_W
W_CAT = OFF_C + 4 * DIFF_W

F32 = jnp.float32
BF16 = jnp.bfloat16


def _dot(a, b):
    return jnp.dot(a, b, preferred_element_type=F32)


def _dot_nt(a, b):
    return lax.dot_general(a, b, (((1,), (1,)), ((), ())), preferred_element_type=F32)


def _alibi_slopes():
    n = MOBA_HEADS + DIFF_HEADS
    s = 2.0 ** (-8.0 * (np.arange(n) + 1) / n)
    diff_idx = np.arange(DIFF_HEADS) * (n // DIFF_HEADS)
    moba_idx = np.setdiff1d(np.arange(n), diff_idx)
    return (jnp.asarray(s[moba_idx], dtype=F32), jnp.asarray(s[diff_idx], dtype=F32))


def _rope_tables():
    d = MLA_ROPE
    freqs = ROPE_THETA ** (-np.arange(0, d, 2, dtype=np.float32) / d)
    ang = np.arange(SEQ, dtype=np.float32)[:, None] * freqs[None, :]
    cos, sin = np.cos(ang), np.sin(ang)
    cos_t = np.zeros((SEQ, LANES), np.float32)
    sin_t = np.zeros((SEQ, LANES), np.float32)
    cos_t[:, :MLA_NOPE] = 1.0
    cos_t[:, MLA_NOPE:MLA_NOPE + d] = np.concatenate([cos, cos], axis=-1)
    sin_t[:, MLA_NOPE:MLA_NOPE + d] = np.concatenate([sin, sin], axis=-1)
    return jnp.asarray(cos_t), jnp.asarray(sin_t)


def _rot_half_cols(w):
    half = MLA_ROPE // 2
    return jnp.concatenate([-w[..., half:], w[..., :half]], axis=-1)


def _proj_kernel(x_ref, whead_ref, wkr_ref, wtail_ref, wq_ref, wqs_ref, wkv_ref, gq_ref, gkv_ref,
                 cos_ref, sin_ref, ktab_ref,
                 aq_ref, ak_ref, avt_ref, az_ref, kmean_ref,
                 bq_ref, bk_ref, bvt_ref, bz_ref,
                 cq_ref, ck_ref, cvt_ref, cz_ref):
    xb = x_ref[...].astype(BF16)
    ktab = ktab_ref[...]
    ones_rows = jnp.where(lax.broadcasted_iota(jnp.int32, (ONES_ROWS, TM), 0) == 0,
                          1.0, 0.0).astype(BF16)

    def store_vt(vt_ref, v, dv, stride=None, offset=0):
        stride = stride or dv
        vt = v.T.astype(BF16)
        for h in range(v.shape[1] // stride):
            base = h * (dv + ONES_ROWS)
            vt_ref[base:base + dv, :] = vt[h * stride + offset:h * stride + offset + dv]
            vt_ref[base + dv:base + dv + ONES_ROWS, :] = ones_rows

    def store_k_aug(k_ref, k):
        for h in range(k.shape[1] // LANES):
            k_ref[:, 2 * h * LANES:(2 * h + 1) * LANES] = (
                k[:, h * LANES:(h + 1) * LANES].astype(BF16))
            k_ref[:, (2 * h + 1) * LANES:(2 * h + 2) * LANES] = ktab

    def proj(off, width):
        if off < OFF_KR:
            return _dot(xb, whead_ref[:, off:off + width])
        if off == OFF_KR:
            return _dot(xb, wkr_ref[...])
        return _dot(xb, wtail_ref[:, off - OFF_BZ:off - OFF_BZ + width])

    cos_t = cos_ref[...]
    sin_t = sin_ref[...]
    cq = proj(OFF_CQ, MLA_Q_LORA)
    ckv = proj(OFF_CKV, MLA_KV_LORA)
    kr = proj(OFF_KR, LANES)
    krot = kr * cos_t + pltpu.roll(kr, LANES - MLA_ROPE, 1) * sin_t
    cqn = cq * lax.rsqrt(jnp.mean(cq * cq, axis=-1, keepdims=True) + 1e-6) * gq_ref[...]
    cqn = cqn.astype(BF16)
    ckvn = ckv * lax.rsqrt(jnp.mean(ckv * ckv, axis=-1, keepdims=True) + 1e-6) * gkv_ref[...]
    ckvn = ckvn.astype(BF16)

    aq_ref[...] = (proj(OFF_A, MOBA_W) * (HEAD_DIM ** -0.5 * LOG2E)).astype(BF16)
    ka = proj(OFF_A + MOBA_W, MOBA_W)
    store_k_aug(ak_ref, ka)
    kmean_ref[0] = jnp.mean(ka, axis=0, keepdims=True)
    store_vt(avt_ref, proj(OFF_A + 2 * MOBA_W, MOBA_W), HEAD_DIM)
    az_ref[...] = proj(OFF_A + 3 * MOBA_W, MOBA_W).astype(BF16)

    qa = _dot(cqn, wq_ref[...])
    qb = _dot(cqn, wqs_ref[...])
    qscale = (MLA_NOPE + MLA_ROPE) ** -0.5 * LOG2E
    cos_q = cos_t * qscale
    sin_q = sin_t * qscale
    heads_per_tile = LANES // MLA_ROPE
    for h in range(MLA_HEADS):
        sl = slice(h * LANES, (h + 1) * LANES)
        qb_tile = qb[:, (h // heads_per_tile) * LANES:(h // heads_per_tile + 1) * LANES]
        shift = (MLA_NOPE - (h % heads_per_tile) * MLA_ROPE) % LANES
        partner = pltpu.roll(qb_tile, shift, 1) if shift else qb_tile
        bq_ref[:, sl] = (qa[:, sl] * cos_q + partner * sin_q).astype(BF16)
    kv = _dot(ckvn, wkv_ref[...])
    nope_lanes = lax.broadcasted_iota(jnp.int32, (TM, LANES), 1) < MLA_NOPE
    for h in range(MLA_HEADS):
        sl = slice(h * LANES, (h + 1) * LANES)
        bk_ref[:, sl] = jnp.where(nope_lanes, kv[:, sl], krot).astype(BF16)
    store_vt(bvt_ref, kv, MLA_V, stride=MLA_NOPE + MLA_V, offset=MLA_NOPE)
    bz_ref[...] = proj(OFF_BZ, MLA_W).astype(BF16)

    cq_ref[...] = (proj(OFF_C, DIFF_W) * (DIFF_QK ** -0.5 * LOG2E)).astype(BF16)
    store_k_aug(ck_ref, proj(OFF_C + DIFF_W, DIFF_W))
    store_vt(cvt_ref, proj(OFF_C + 2 * DIFF_W, DIFF_W), DIFF_V)
    cz_ref[...] = proj(OFF_C + 3 * DIFF_W, DIFF_W).astype(BF16)


def _layer_spec(stacked, li):
    zeros = (0,) * (stacked.ndim - 1)
    return pl.BlockSpec((None,) + stacked.shape[1:], lambda *_: (li,) + zeros,
                        pipeline_mode=pl.Buffered(1))


def _proj_call(li, x2, wcat, wq, wqs, wkv, gq, gkv, cos_t, sin_t, ktab):
    n_tiles = N_TOK // TM
    pos_tiles = SEQ // TM
    row = lambda w: pl.BlockSpec((TM, w), lambda i: (i, 0))
    tab = pl.BlockSpec((TM, LANES), lambda i: (i % pos_tiles, 0))
    out_shapes = []
    out_specs = []

    def add(width):
        out_shapes.append(jax.ShapeDtypeStruct((N_TOK, width), BF16))
        out_specs.append(row(width))

    def add_vt(heads, dv):
        rows = heads * (dv + ONES_ROWS)
        out_shapes.append(jax.ShapeDtypeStruct((BATCH, rows, SEQ), BF16))
        out_specs.append(pl.BlockSpec((None, rows, TM),
                                      lambda i: (i // pos_tiles, 0, i % pos_tiles)))

    add(MOBA_W); add(2 * MOBA_W); add_vt(MOBA_HEADS, HEAD_DIM); add(MOBA_W)
    out_shapes.append(jax.ShapeDtypeStruct((n_tiles, 1, MOBA_W), F32))
    out_specs.append(pl.BlockSpec((1, 1, MOBA_W), lambda i: (i, 0, 0)))
    add(MLA_HEADS * LANES); add(MLA_HEADS * LANES); add_vt(MLA_HEADS, MLA_V); add(MLA_W)
    add(DIFF_W); add(2 * DIFF_W); add_vt(DIFF_HEADS, DIFF_V); add(DIFF_W)
    return pl.pallas_call(
        _proj_kernel,
        grid=(n_tiles,),
        in_specs=[row(D_MODEL)] + [_layer_spec(w, li) for w in (*wcat, wq, wqs, wkv, gq, gkv)]
                 + [tab, tab, tab],
        out_specs=out_specs,
        out_shape=out_shapes,
        compiler_params=pltpu.CompilerParams(
            dimension_semantics=("arbitrary",), vmem_limit_bytes=VMEM_LIMIT),
        name="proj",
    )(x2, *wcat, wq, wqs, wkv, gq, gkv, cos_t, sin_t, ktab)


AUG_POS = 8
ONES_ROWS = 16


def _split3(v):
    hi = v.astype(BF16).astype(F32)
    mid = (v - hi).astype(BF16).astype(F32)
    lo = (v - hi - mid).astype(BF16).astype(F32)
    return hi, mid, lo


def _key_aug_table():
    kpos = np.arange(SEQ)
    t = np.zeros((SEQ, LANES), np.float32)
    t[kpos, kpos // MOBA_BLOCK] = 1.0
    t[:, AUG_POS:AUG_POS + 3] = (kpos // 256)[:, None]
    t[:, AUG_POS + 3:AUG_POS + 6] = (kpos % 256)[:, None]
    return jnp.asarray(t).astype(BF16)


def _query_aug_rows(slopes, group):
    s2 = slopes * LOG2E
    pieces = jnp.stack(_split3(s2 * 256.0) + _split3(s2), axis=-1)
    rows = jnp.zeros((slopes.shape[0], LANES), F32).at[:, AUG_POS:AUG_POS + 6].set(pieces)
    rows = rows.reshape(slopes.shape[0] // group, group, LANES)
    return jnp.pad(rows, ((0, 0), (0, 8 - group), (0, 0)))


def _flash_chains(q_list, k_tile, vt_tile, ti, q_first=None):
    n = len(q_list)
    nsc = n * NSUB
    key_i = lax.broadcasted_iota(jnp.int32, (TQ, QSUB), 0)
    qry_i = lax.broadcasted_iota(jnp.int32, (TQ, QSUB), 1)

    q_cache = {}

    def q_sub(sc):
        c, d = divmod(sc, NSUB)
        if c not in q_cache:
            q = q_list[c]
            q_cache[c] = q() if callable(q) else q
        return q_cache[c][d * QSUB:(d + 1) * QSUB]

    masks = [(key_i <= qry_i + d * QSUB)[:(d + 1) * QSUB] for d in range(NSUB)]
    items = [(sc, ti * TQ, (sc % NSUB + 1) * QSUB, masks[sc % NSUB]) for sc in range(nsc)]
    for j in range(ti * TQ // TK):
        items += [(sc, j * TK, TK, None) for sc in range(nsc)]

    m = [None] * nsc
    acc = [None] * nsc
    live = {}

    def scores(i):
        sc, start, size, mask = items[i]
        if q_first is not None and m[sc] is None and sc % NSUB == 0:
            q = q_first[sc // NSUB][:QSUB]
        else:
            q = q_sub(sc)
        s = _dot_nt(k_tile(sc // NSUB, start, size), q)
        if mask is not None:
            s = jnp.where(mask, s, NEG)
        smax = jnp.max(s, axis=0, keepdims=True)
        m_new = smax if m[sc] is None else jnp.maximum(m[sc], smax)
        a = None if m[sc] is None else jnp.exp2(m[sc] - m_new)
        m[sc] = m_new
        live[i] = (s, m_new, a)

    def probs(i):
        s, m_new, a = live[i]
        live[i] = (jnp.exp2(s - m_new).astype(BF16), a)

    def values(i):
        sc, start, size, _ = items[i]
        p, a = live.pop(i)
        pv = _dot(vt_tile(sc // NSUB, start, size), p)
        acc[sc] = pv if a is None else a * acc[sc] + pv

    for r in range(0, len(items), nsc):
        rnd = range(r, r + nsc)
        for i in rnd:
            scores(i)
        for i in rnd:
            probs(i)
            values(i)
    return [jnp.concatenate(acc[c * NSUB:(c + 1) * NSUB], axis=1) for c in range(n)]


def _per_query_tile(ti, fn):
    for t in range(SEQ // TQ):
        pl.when(ti == t)(functools.partial(fn, t))


MOBA_NV = HEAD_DIM + ONES_ROWS
MOBA_GROUP = 4


def _moba_kernel(qaug_ref, q_ref, k_ref, vt_ref, kmean_ref, o_ref):
    ti = pl.program_id(2)
    lane_q = lax.broadcasted_iota(jnp.int32, (TQ, LANES), 1)
    heads = range(MOBA_GROUP)
    lane_h = lax.broadcasted_iota(jnp.int32, (QSUB, LANES), 1)
    blk = lax.broadcasted_iota(jnp.int32, (N_KV_BLOCKS, QSUB), 0)
    pad_rows = jnp.zeros((LANES - N_KV_BLOCKS, QSUB), F32)
    pair_lanes = lambda hh: slice((hh // 2) * LANES, (hh // 2 + 1) * LANES)
    qm = []
    for hh in heads:
        q2 = q_ref[:, pair_lanes(hh)]
        qm.append(jnp.where((lane_q // HEAD_DIM) == hh % 2, q2, jnp.zeros_like(q2)))
    q_plain = [jnp.concatenate(
        [qm[hh], jnp.broadcast_to(qaug_ref[hh:hh + 1, :], (TQ, LANES)).astype(BF16)], axis=1)
        for hh in heads]

    def select_lanes(t, hh):
        need_rank = NSUB * (t + 1) - 1 > MOBA_TOPK
        if need_rank:
            km = jnp.concatenate(
                [kmean_ref[:, pair_lanes(hh)], jnp.zeros((LANES - N_KV_BLOCKS, LANES), F32)],
                axis=0).astype(BF16)
            gt = _dot_nt(km, qm[hh])
        qx = []
        for d in range(NSUB):
            qb = NSUB * t + d
            if qb <= MOBA_TOPK:
                attend = blk <= qb
            else:
                g = gt[:N_KV_BLOCKS, d * QSUB:(d + 1) * QSUB]
                cnt = jnp.zeros((N_KV_BLOCKS, QSUB), F32)
                for mm in range(qb):
                    gm = g[mm:mm + 1, :]
                    beats = (gm > g) | ((gm == g) & (mm < blk))
                    cnt = cnt + jnp.where(beats, 1.0, 0.0)
                attend = ((blk < qb) & (cnt < MOBA_TOPK)) | (blk == qb)
            selb_t = jnp.where(attend, 0.0, NEG)
            selb = jnp.concatenate([selb_t, pad_rows], axis=0).T
            qx.append(jnp.where(lane_h < N_KV_BLOCKS, selb, qaug_ref[hh:hh + 1, :]).astype(BF16))
        return jnp.concatenate(qx, axis=0)

    def finish(t):
        q_list = [functools.partial(
            lambda hh: jnp.concatenate([qm[hh], select_lanes(t, hh)], axis=1), hh)
            for hh in heads]
        acc = _flash_chains(
            q_list,
            lambda c, st, size: k_ref[pl.ds(st, size), (c // 2) * 2 * LANES:(c // 2 + 1) * 2 * LANES],
            lambda c, st, size: vt_ref[c * MOBA_NV:(c + 1) * MOBA_NV, pl.ds(st, size)],
            t, q_first=q_plain)
        for pr in range(MOBA_GROUP // 2):
            out_t = jnp.concatenate(
                [a[:HEAD_DIM] / a[HEAD_DIM:HEAD_DIM + 1] for a in acc[2 * pr:2 * pr + 2]],
                axis=0)
            o_ref[:, pr * LANES:(pr + 1) * LANES] = out_t.T.astype(BF16)

    _per_query_tile(ti, finish)


def _moba_call(qaug, q, k, vt, kmean):
    nq = SEQ // TQ
    g = MOBA_GROUP
    return pl.pallas_call(
        _moba_kernel,
        grid=(BATCH, MOBA_HEADS // g, nq),
        in_specs=[
            pl.BlockSpec((None, 8, LANES), lambda b, h, i: (h, 0, 0)),
            pl.BlockSpec((TQ, g * HEAD_DIM), lambda b, h, i: (b * nq + i, h)),
            pl.BlockSpec((SEQ, g * LANES), lambda b, h, i: (b, h)),
            pl.BlockSpec((None, g * MOBA_NV, SEQ), lambda b, h, i: (b, h, 0)),
            pl.BlockSpec((None, N_KV_BLOCKS, g * HEAD_DIM), lambda b, h, i: (b, 0, h)),
        ],
        out_specs=pl.BlockSpec((TQ, g * HEAD_DIM), lambda b, h, i: (b * nq + i, h)),
        out_shape=jax.ShapeDtypeStruct((N_TOK, MOBA_W), BF16),
        compiler_params=pltpu.CompilerParams(
            dimension_semantics=("arbitrary", "arbitrary", "arbitrary"),
            vmem_limit_bytes=VMEM_LIMIT),
        name="moba",
    )(qaug, q, k, vt, kmean)


MLA_GROUP = 4


MLA_NV = MLA_V + ONES_ROWS


def _mla_kernel(q_ref, k_ref, vt_ref, o_ref):
    ti = pl.program_id(2)
    q_list = [q_ref[:, c * LANES:(c + 1) * LANES] for c in range(MLA_GROUP)]

    def finish(t):
        acc = _flash_chains(
            q_list,
            lambda c, st, size: k_ref[pl.ds(st, size), c * LANES:(c + 1) * LANES],
            lambda c, st, size: vt_ref[c * MLA_NV:(c + 1) * MLA_NV, pl.ds(st, size)],
            t)
        for pr in range(MLA_GROUP // 2):
            out_t = jnp.concatenate(
                [a[:MLA_V] / a[MLA_V:MLA_V + 1] for a in acc[2 * pr:2 * pr + 2]], axis=0)
            o_ref[:, pr * LANES:(pr + 1) * LANES] = out_t.T.astype(BF16)

    _per_query_tile(ti, finish)


def _mla_call(q, k, vt):
    nq = SEQ // TQ
    g = MLA_GROUP
    return pl.pallas_call(
        _mla_kernel,
        grid=(BATCH, MLA_HEADS // g, nq),
        in_specs=[
            pl.BlockSpec((TQ, g * LANES), lambda b, h, i: (b * nq + i, h)),
            pl.BlockSpec((SEQ, g * LANES), lambda b, h, i: (b, h)),
            pl.BlockSpec((None, g * MLA_NV, SEQ), lambda b, h, i: (b, h, 0)),
        ],
        out_specs=pl.BlockSpec((TQ, g * MLA_V), lambda b, h, i: (b * nq + i, h)),
        out_shape=jax.ShapeDtypeStruct((N_TOK, MLA_W), BF16),
        compiler_params=pltpu.CompilerParams(
            dimension_semantics=("arbitrary", "arbitrary", "arbitrary"),
            vmem_limit_bytes=VMEM_LIMIT),
        name="mla",
    )(q, k, vt)


DIFF_GROUP = 4


DIFF_NV = DIFF_V + ONES_ROWS


def _diff_kernel(lam_init, qaug_ref, q_ref, k_ref, vt_ref, lam_ref, g_ref, o_ref):
    ti = pl.program_id(2)
    lane_q = lax.broadcasted_iota(jnp.int32, (TQ, LANES), 1)
    q_list = []
    for hh in range(DIFF_GROUP):
        q2 = q_ref[:, hh * LANES:(hh + 1) * LANES]
        qx = jnp.broadcast_to(qaug_ref[hh:hh + 1, :], (TQ, LANES)).astype(BF16)
        for c in range(2):
            qm = jnp.where((lane_q // DIFF_QK) == c, q2, jnp.zeros_like(q2))
            q_list.append(jnp.concatenate([qm, qx], axis=1))

    lf = lam_ref[...]
    e1 = jnp.exp(jnp.sum(lf[0:1, :] * lf[1:2, :], axis=-1, keepdims=True))
    e2 = jnp.exp(jnp.sum(lf[2:3, :] * lf[3:4, :], axis=-1, keepdims=True))
    lam = e1 - e2 + lam_init

    def finish(t):
        acc = _flash_chains(
            q_list,
            lambda c, st, size: k_ref[pl.ds(st, size),
                                      (c // 2) * 2 * LANES:(c // 2 + 1) * 2 * LANES],
            lambda c, st, size: vt_ref[(c // 2) * DIFF_NV:(c // 2 + 1) * DIFF_NV,
                                       pl.ds(st, size)],
            t)
        for hh in range(DIFF_GROUP):
            a0, a1 = acc[2 * hh], acc[2 * hh + 1]
            o0 = a0[:DIFF_V] / a0[DIFF_V:DIFF_V + 1]
            o1 = a1[:DIFF_V] / a1[DIFF_V:DIFF_V + 1]
            y = (o0 - lam * o1).T
            y = y * lax.rsqrt(jnp.mean(y * y, axis=-1, keepdims=True) + 1e-5) * g_ref[...]
            o_ref[:, hh * LANES:(hh + 1) * LANES] = (y * (1.0 - lam_init)).astype(BF16)

    _per_query_tile(ti, finish)


def _diff_call(li, lam_init, qaug, q, k, vt, lam_p, subln_g):
    nq = SEQ // TQ
    g = DIFF_GROUP
    return pl.pallas_call(
        functools.partial(_diff_kernel, lam_init),
        grid=(BATCH, DIFF_HEADS // g, nq),
        in_specs=[
            pl.BlockSpec((None, 8, LANES), lambda b, h, i: (h, 0, 0)),
            pl.BlockSpec((TQ, g * LANES), lambda b, h, i: (b * nq + i, h)),
            pl.BlockSpec((SEQ, g * 2 * LANES), lambda b, h, i: (b, h)),
            pl.BlockSpec((None, g * DIFF_NV, SEQ), lambda b, h, i: (b, h, 0)),
            _layer_spec(lam_p, li),
            _layer_spec(subln_g, li),
        ],
        out_specs=pl.BlockSpec((TQ, g * LANES), lambda b, h, i: (b * nq + i, h)),
        out_shape=jax.ShapeDtypeStruct((N_TOK, DIFF_W), BF16),
        compiler_params=pltpu.CompilerParams(
            dimension_semantics=("arbitrary", "arbitrary", "arbitrary"),
            vmem_limit_bytes=VMEM_LIMIT),
        name="diff",
    )(qaug, q, k, vt, lam_p, subln_g)


def _merge_kernel(x_ref, ya_ref, yb_ref, yc_ref, za_ref, zb_ref, zc_ref, p_ref,
                  wm_ref, bm_ref, wa_ref, wb_ref, wc_ref, wo_ref, wpg_ref, wp_ref,
                  lng_ref, lnb_ref, o_ref):
    subs = [slice(s * TM_SUB, (s + 1) * TM_SUB) for s in range(TM_MERGE // TM_SUB)]
    branches = ((ya_ref, za_ref, wa_ref), (yb_ref, zb_ref, wb_ref), (yc_ref, zc_ref, wc_ref))

    def merge_branches(rows):
        xb = x_ref[rows, :].astype(BF16)
        merged = jnp.zeros((TM_SUB, D_MODEL), F32)
        for i, (y_ref, z_ref, w_ref) in enumerate(branches):
            sl = slice(i * D_MODEL, (i + 1) * D_MODEL)
            gate = jax.nn.sigmoid(_dot(xb, wm_ref[:, sl]) + bm_ref[:, sl])
            z = z_ref[rows, :].astype(F32)
            y = y_ref[rows, :].astype(F32) * (z * jax.nn.sigmoid(z))
            merged = merged + gate * _dot(y.astype(BF16), w_ref[...])
        return merged

    r = []
    for rows in subs:
        merged = merge_branches(rows)
        ple = _dot(p_ref[rows, :].astype(BF16), wp_ref[...])
        ri = ALPHA * x_ref[rows, :] + _dot(merged.astype(BF16), wo_ref[...])
        r.append(ri + jax.nn.sigmoid(_dot(ri.astype(BF16), wpg_ref[...])) * ple)
    for rows, ri in zip(subs, r):
        mu = jnp.mean(ri, axis=-1, keepdims=True)
        d = ri - mu
        var = jnp.mean(d * d, axis=-1, keepdims=True)
        o_ref[rows, :] = d * lax.rsqrt(var + NORM_EPS) * lng_ref[...] + lnb_ref[...]


def _merge_call(li, x2, ya, yb, yc, za, zb, zc, p2, wm, bm, wa, wb, wc, wo, wpg, wp, lng, lnb):
    n_tiles = N_TOK // TM_MERGE
    row = lambda w: pl.BlockSpec((TM_MERGE, w), lambda i: (i, 0))
    p_row = pl.BlockSpec((TM_MERGE, PLE_DIM), lambda i: (li * n_tiles + i, 0))
    consts = (wm, bm, wa, wb, wc, wo, wpg, wp, lng, lnb)
    return pl.pallas_call(
        _merge_kernel,
        grid=(n_tiles,),
        in_specs=[row(D_MODEL)] + [row(MOBA_W)] * 6 + [p_row]
                 + [_layer_spec(c, li) for c in consts],
        out_specs=row(D_MODEL),
        out_shape=jax.ShapeDtypeStruct((N_TOK, D_MODEL), F32),
        compiler_params=pltpu.CompilerParams(
            dimension_semantics=("arbitrary",), vmem_limit_bytes=VMEM_LIMIT),
        name="merge",
    )(x2, ya, yb, yc, za, zb, zc, p2, *consts)


def _prep_proj_weights(w_in, w_uq):
    pts = np.cumsum([MOBA_W] * 4 + [MLA_Q_LORA, MLA_KV_LORA, MLA_ROPE, MLA_W] + [DIFF_W] * 4)
    head, kr, tail = w_in[..., :pts[5]], w_in[..., pts[5]:pts[6]], w_in[..., pts[6]:]
    z_lo = jnp.zeros(w_in.shape[:-1] + (MLA_NOPE,), F32)
    wkr = jnp.concatenate([z_lo, kr, _rot_half_cols(kr)], axis=-1)
    wcat = (head.astype(BF16), wkr.astype(BF16), tail.astype(BF16))

    dq = MLA_NOPE + MLA_ROPE
    uq = w_uq.reshape(DEPTH, MLA_Q_LORA, MLA_HEADS, dq)
    pad_q = jnp.zeros((DEPTH, MLA_Q_LORA, MLA_HEADS, LANES - dq), F32)
    wq = jnp.concatenate([uq, pad_q], axis=-1)
    wqs = _rot_half_cols(uq[..., MLA_NOPE:])
    flat = lambda w: w.reshape(DEPTH, MLA_Q_LORA, -1).astype(BF16)
    return wcat, flat(wq), flat(wqs)


def kernel(x, p, w_in, mla_q_norm_g, mla_kv_norm_g, mla_w_uq, mla_w_ukv, diff_lambda, diff_subln_g, w_branch_a, w_branch_b, w_branch_c, w_merge, b_merge, w_out, ln_g, ln_b, w_ple_gate, w_ple):
    slopes_a, slopes_c = _alibi_slopes()
    qaug_a = _query_aug_rows(slopes_a, MOBA_GROUP)
    qaug_c = _query_aug_rows(slopes_c, DIFF_GROUP)
    ktab = _key_aug_table()
    cos_t, sin_t = _rope_tables()
    bf = lambda w: w.astype(BF16)
    vec = lambda v: v[:, None, :]
    wcat, wq, wqs = _prep_proj_weights(w_in, mla_w_uq)
    wkv = bf(mla_w_ukv)
    merge_consts = (bf(w_merge), vec(b_merge), bf(w_branch_a), bf(w_branch_b), bf(w_branch_c),
                    bf(w_out), bf(w_ple_gate), bf(w_ple), vec(ln_g), vec(ln_b))
    gq, gkv, subln_g = vec(mla_q_norm_g), vec(mla_kv_norm_g), vec(diff_subln_g)
    p2 = p.reshape(DEPTH * N_TOK, PLE_DIM)
    h = x.reshape(N_TOK, D_MODEL)
    for i in range(DEPTH):
        (aq, ak, avt, az, kmean, bq, bk, bvt, bz, cq, ck, cvt, cz) = _proj_call(
            i, h, wcat, wq, wqs, wkv, gq, gkv, cos_t, sin_t, ktab)
        kmean = kmean.reshape(BATCH, N_KV_BLOCKS, MOBA_W)
        ya = _moba_call(qaug_a, aq, ak, avt, kmean)
        yb = _mla_call(bq, bk, bvt)
        lam_init = 0.8 - 0.6 * math.exp(-0.3 * i)
        yc = _diff_call(i, lam_init, qaug_c, cq, ck, cvt, diff_lambda, subln_g)
        h = _merge_call(i, h, ya, yb, yc, az, bz, cz, p2, *merge_consts)
    return h.reshape(BATCH, SEQ, D_MODEL)
```

```python
import functools
import math

import numpy as np
import jax
import jax.numpy as jnp
from jax import lax
from jax.experimental import pallas as pl
from jax.experimental.pallas import tpu as pltpu

D_MODEL = 1024
BATCH = 8
SEQ = 2048
DEPTH = 2
MOBA_HEADS = 8
HEAD_DIM = 64
MOBA_BLOCK = 256
MOBA_TOPK = 3
MLA_HEADS = 8
MLA_Q_LORA = 384
MLA_KV_LORA = 256
MLA_NOPE = 64
MLA_ROPE = 32
MLA_V = 64
ROPE_THETA = 10000.0
DIFF_HEADS = 4
DIFF_QK = 64
DIFF_V = 2 * DIFF_QK
PLE_DIM = 256
NORM_EPS = 1e-5
NEG = -1e30
N_BRANCH = 3
ALPHA = (2 * DEPTH) ** 0.25
LOG2E = math.log2(math.e)

MOBA_W = MOBA_HEADS * HEAD_DIM
MLA_W = MLA_HEADS * MLA_V
DIFF_W = DIFF_HEADS * DIFF_V
N_TOK = BATCH * SEQ
N_KV_BLOCKS = SEQ // MOBA_BLOCK

LANES = 128
VMEM_LIMIT = 54 * 1024 * 1024
TM = 512
TM_MERGE = 512
TM_SUB = MOBA_BLOCK
TQ = 1024
TK = 512
QSUB = MOBA_BLOCK
NSUB = TQ // QSUB

OFF_A = 0
OFF_CQ = OFF_A + 4 * MOBA_W
OFF_CKV = OFF_CQ + MLA_Q_LORA
OFF_KR = OFF_CKV + MLA_KV_LORA
OFF_BZ = OFF_KR + LANES
OFF_C = OFF_BZ + MLA_W
W_CAT = OFF_C + 4 * DIFF_W

F32 = jnp.float32
BF16 = jnp.bfloat16


def _dot(a, b):
    return jnp.dot(a, b, preferred_element_type=F32)


def _dot_nt(a, b):
    return lax.dot_general(a, b, (((1,), (1,)), ((), ())), preferred_element_type=F32)


def _alibi_slopes():
    n = MOBA_HEADS + DIFF_HEADS
    s = 2.0 ** (-8.0 * (np.arange(n) + 1) / n)
    diff_idx = np.arange(DIFF_HEADS) * (n // DIFF_HEADS)
    moba_idx = np.setdiff1d(np.arange(n), diff_idx)
    return (jnp.asarray(s[moba_idx], dtype=F32), jnp.asarray(s[diff_idx], dtype=F32))


def _rope_tables():
    d = MLA_ROPE
    freqs = ROPE_THETA ** (-np.arange(0, d, 2, dtype=np.float32) / d)
    ang = np.arange(SEQ, dtype=np.float32)[:, None] * freqs[None, :]
    cos, sin = np.cos(ang), np.sin(ang)
    cos_t = np.zeros((SEQ, LANES), np.float32)
    sin_t = np.zeros((SEQ, LANES), np.float32)
    cos_t[:, :MLA_NOPE] = 1.0
    cos_t[:, MLA_NOPE:MLA_NOPE + d] = np.concatenate([cos, cos], axis=-1)
    sin_t[:, MLA_NOPE:MLA_NOPE + d] = np.concatenate([sin, sin], axis=-1)
    return jnp.asarray(cos_t), jnp.asarray(sin_t)


def _rot_half_cols(w):
    half = MLA_ROPE // 2
    return jnp.concatenate([-w[..., half:], w[..., :half]], axis=-1)


def _proj_kernel(x_ref, whead_ref, wkr_ref, wtail_ref, wq_ref, wqs_ref, wkv_ref, gq_ref, gkv_ref,
                 cos_ref, sin_ref, ktab_ref,
                 aq_ref, ak_ref, avt_ref, az_ref, kmean_ref,
                 bq_ref, bk_ref, bvt_ref, bz_ref,
                 cq_ref, ck_ref, cvt_ref, cz_ref):
    for st in range(TM // TM_SUB):
        _proj_sub_tile(st, x_ref, whead_ref, wkr_ref, wtail_ref, wq_ref, wqs_ref, wkv_ref,
                       gq_ref, gkv_ref, cos_ref, sin_ref, ktab_ref,
                       aq_ref, ak_ref, avt_ref, az_ref, kmean_ref,
                       bq_ref, bk_ref, bvt_ref, bz_ref, cq_ref, ck_ref, cvt_ref, cz_ref)


def _proj_sub_tile(st, x_ref, whead_ref, wkr_ref, wtail_ref, wq_ref, wqs_ref, wkv_ref,
                   gq_ref, gkv_ref, cos_ref, sin_ref, ktab_ref,
                   aq_ref, ak_ref, avt_ref, az_ref, kmean_ref,
                   bq_ref, bk_ref, bvt_ref, bz_ref, cq_ref, ck_ref, cvt_ref, cz_ref):
    rows = slice(st * TM_SUB, (st + 1) * TM_SUB)
    xb = x_ref[rows, :].astype(BF16)
    ktab = ktab_ref[rows, :]
    ones_rows = jnp.where(lax.broadcasted_iota(jnp.int32, (ONES_ROWS, TM_SUB), 0) == 0,
                          1.0, 0.0).astype(BF16)

    def store_vt(vt_ref, v, dv, stride=None, offset=0):
        stride = stride or dv
        vt = v.T.astype(BF16)
        for h in range(v.shape[1] // stride):
            base = h * (dv + ONES_ROWS)
            vt_ref[base:base + dv, rows] = vt[h * stride + offset:h * stride + offset + dv]
            vt_ref[base + dv:base + dv + ONES_ROWS, rows] = ones_rows

    def store_k_aug(k_ref, k):
        for h in range(k.shape[1] // LANES):
            k_ref[rows, 2 * h * LANES:(2 * h + 1) * LANES] = (
                k[:, h * LANES:(h + 1) * LANES].astype(BF16))
            k_ref[rows, (2 * h + 1) * LANES:(2 * h + 2) * LANES] = ktab

    def silu(z):
        return z * jax.nn.sigmoid(z)

    def proj(off, width):
        if off < OFF_KR:
            return _dot(xb, whead_ref[:, off:off + width])
        if off == OFF_KR:
            return _dot(xb, wkr_ref[...])
        return _dot(xb, wtail_ref[:, off - OFF_BZ:off - OFF_BZ + width])

    cos_t = cos_ref[rows, :]
    sin_t = sin_ref[rows, :]
    cq = proj(OFF_CQ, MLA_Q_LORA)
    ckv = proj(OFF_CKV, MLA_KV_LORA)
    kr = proj(OFF_KR, LANES)
    krot = kr * cos_t + pltpu.roll(kr, LANES - MLA_ROPE, 1) * sin_t
    cqn = cq * lax.rsqrt(jnp.mean(cq * cq, axis=-1, keepdims=True) + 1e-6) * gq_ref[...]
    cqn = cqn.astype(BF16)
    ckvn = ckv * lax.rsqrt(jnp.mean(ckv * ckv, axis=-1, keepdims=True) + 1e-6) * gkv_ref[...]
    ckvn = ckvn.astype(BF16)

    aq_ref[rows, :] = (proj(OFF_A, MOBA_W) * (HEAD_DIM ** -0.5 * LOG2E)).astype(BF16)
    ka = proj(OFF_A + MOBA_W, MOBA_W)
    store_k_aug(ak_ref, ka)
    kmean_ref[st] = jnp.mean(ka, axis=0, keepdims=True)
    store_vt(avt_ref, proj(OFF_A + 2 * MOBA_W, MOBA_W), HEAD_DIM)
    az_ref[rows, :] = silu(proj(OFF_A + 3 * MOBA_W, MOBA_W)).astype(BF16)

    qa = _dot(cqn, wq_ref[...])
    qb = _dot(cqn, wqs_ref[...])
    qscale = (MLA_NOPE + MLA_ROPE) ** -0.5 * LOG2E
    cos_q = cos_t * qscale
    sin_q = sin_t * qscale
    heads_per_tile = LANES // MLA_ROPE
    for h in range(MLA_HEADS):
        sl = slice(h * LANES, (h + 1) * LANES)
        qb_tile = qb[:, (h // heads_per_tile) * LANES:(h // heads_per_tile + 1) * LANES]
        shift = (MLA_NOPE - (h % heads_per_tile) * MLA_ROPE) % LANES
        partner = pltpu.roll(qb_tile, shift, 1) if shift else qb_tile
        bq_ref[rows, sl] = (qa[:, sl] * cos_q + partner * sin_q).astype(BF16)
    kv = _dot(ckvn, wkv_ref[...])
    nope_lanes = lax.broadcasted_iota(jnp.int32, (TM_SUB, LANES), 1) < MLA_NOPE
    for h in range(MLA_HEADS):
        sl = slice(h * LANES, (h + 1) * LANES)
        bk_ref[rows, sl] = jnp.where(nope_lanes, kv[:, sl], krot).astype(BF16)
    store_vt(bvt_ref, kv, MLA_V, stride=MLA_NOPE + MLA_V, offset=MLA_NOPE)
    bz_ref[rows, :] = silu(proj(OFF_BZ, MLA_W)).astype(BF16)

    cq_ref[rows, :] = (proj(OFF_C, DIFF_W) * (DIFF_QK ** -0.5 * LOG2E)).astype(BF16)
    store_k_aug(ck_ref, proj(OFF_C + DIFF_W, DIFF_W))
    store_vt(cvt_ref, proj(OFF_C + 2 * DIFF_W, DIFF_W), DIFF_V)
    cz_ref[rows, :] = silu(proj(OFF_C + 3 * DIFF_W, DIFF_W)).astype(BF16)


def _layer_spec(stacked, li):
    zeros = (0,) * (stacked.ndim - 1)
    return pl.BlockSpec((None,) + stacked.shape[1:], lambda *_: (li,) + zeros,
                        pipeline_mode=pl.Buffered(1))


def _proj_call(li, x2, wcat, wq, wqs, wkv, gq, gkv, cos_t, sin_t, ktab):
    n_tiles = N_TOK // TM
    pos_tiles = SEQ // TM
    row = lambda w: pl.BlockSpec((TM, w), lambda i: (i, 0))
    tab = pl.BlockSpec((TM, LANES), lambda i: (i % pos_tiles, 0))
    out_shapes = []
    out_specs = []

    def add(width):
        out_shapes.append(jax.ShapeDtypeStruct((N_TOK, width), BF16))
        out_specs.append(row(width))

    def add_vt(heads, dv):
        rows = heads * (dv + ONES_ROWS)
        out_shapes.append(jax.ShapeDtypeStruct((BATCH, rows, SEQ), BF16))
        out_specs.append(pl.BlockSpec((None, rows, TM),
                                      lambda i: (i // pos_tiles, 0, i % pos_tiles)))

    add(MOBA_W); add(2 * MOBA_W); add_vt(MOBA_HEADS, HEAD_DIM); add(MOBA_W)
    subs = TM // TM_SUB
    out_shapes.append(jax.ShapeDtypeStruct((n_tiles * subs, 1, MOBA_W), F32))
    out_specs.append(pl.BlockSpec((subs, 1, MOBA_W), lambda i: (i, 0, 0)))
    add(MLA_HEADS * LANES); add(MLA_HEADS * LANES); add_vt(MLA_HEADS, MLA_V); add(MLA_W)
    add(DIFF_W); add(2 * DIFF_W); add_vt(DIFF_HEADS, DIFF_V); add(DIFF_W)
    return pl.pallas_call(
        _proj_kernel,
        grid=(n_tiles,),
        in_specs=[row(D_MODEL),
                  pl.BlockSpec((None, D_MODEL, OFF_KR), lambda i: (li, 0, 0),
                               pipeline_mode=pl.Buffered(1))]
                 + [_layer_spec(w, li) for w in (*wcat[1:], wq, wqs, wkv, gq, gkv)]
                 + [tab, tab, tab],
        out_specs=out_specs,
        out_shape=out_shapes,
        compiler_params=pltpu.CompilerParams(
            dimension_semantics=("arbitrary",), vmem_limit_bytes=VMEM_LIMIT),
        name="proj",
    )(x2, *wcat, wq, wqs, wkv, gq, gkv, cos_t, sin_t, ktab)


AUG_POS = 8
ONES_ROWS = 16


def _split3(v):
    hi = v.astype(BF16).astype(F32)
    mid = (v - hi).astype(BF16).astype(F32)
    lo = (v - hi - mid).astype(BF16).astype(F32)
    return hi, mid, lo


def _key_aug_table():
    kpos = np.arange(SEQ)
    t = np.zeros((SEQ, LANES), np.float32)
    t[kpos, kpos // MOBA_BLOCK] = 1.0
    t[:, AUG_POS:AUG_POS + 3] = (kpos // 256)[:, None]
    t[:, AUG_POS + 3:AUG_POS + 6] = (kpos % 256)[:, None]
    return jnp.asarray(t).astype(BF16)


def _query_aug_rows(slopes, group):
    s2 = slopes * LOG2E
    pieces = jnp.stack(_split3(s2 * 256.0) + _split3(s2), axis=-1)
    rows = jnp.zeros((slopes.shape[0], LANES), F32).at[:, AUG_POS:AUG_POS + 6].set(pieces)
    rows = rows.reshape(slopes.shape[0] // group, group, LANES)
    return jnp.pad(rows, ((0, 0), (0, 8 - group), (0, 0)))


def _flash_chains(q_list, k_tile, vt_tile, ti, q_first=None):
    n = len(q_list)
    nsc = n * NSUB
    key_i = lax.broadcasted_iota(jnp.int32, (TQ, QSUB), 0)
    qry_i = lax.broadcasted_iota(jnp.int32, (TQ, QSUB), 1)

    q_cache = {}

    def q_sub(sc):
        c, d = divmod(sc, NSUB)
        if c not in q_cache:
            q = q_list[c]
            q_cache[c] = q() if callable(q) else q
        return q_cache[c][d * QSUB:(d + 1) * QSUB]

    masks = [(key_i <= qry_i + d * QSUB)[:(d + 1) * QSUB] for d in range(NSUB)]
    items = [(sc, ti * TQ, (sc % NSUB + 1) * QSUB, masks[sc % NSUB]) for sc in range(nsc)]
    for j in range(ti * TQ // TK):
        items += [(sc, j * TK, TK, None) for sc in range(nsc)]

    m = [None] * nsc
    acc = [None] * nsc
    live = {}

    def scores(i):
        sc, start, size, mask = items[i]
        if q_first is not None and m[sc] is None and sc % NSUB == 0:
            q = q_first[sc // NSUB][:QSUB]
        else:
            q = q_sub(sc)
        s = _dot_nt(k_tile(sc // NSUB, start, size), q)
        if mask is not None:
            s = jnp.where(mask, s, NEG)
        smax = jnp.max(s, axis=0, keepdims=True)
        m_new = smax if m[sc] is None else jnp.maximum(m[sc], smax)
        a = None if m[sc] is None else jnp.exp2(m[sc] - m_new)
        m[sc] = m_new
        live[i] = (s, m_new, a)

    def probs(i):
        s, m_new, a = live[i]
        live[i] = (jnp.exp2(s - m_new).astype(BF16), a)

    def values(i):
        sc, start, size, _ = items[i]
        p, a = live.pop(i)
        pv = _dot(vt_tile(sc // NSUB, start, size), p)
        acc[sc] = pv if a is None else a * acc[sc] + pv

    for r in range(0, len(items), nsc):
        rnd = range(r, r + nsc)
        for i in rnd:
            scores(i)
        for i in rnd:
            probs(i)
            values(i)
    return [jnp.concatenate(acc[c * NSUB:(c + 1) * NSUB], axis=1) for c in range(n)]


def _per_query_tile(ti, fn):
    for t in range(SEQ // TQ):
        pl.when(ti == t)(functools.partial(fn, t))


MOBA_NV = HEAD_DIM + ONES_ROWS
MOBA_GROUP = 4


def _moba_kernel(qaug_ref, q_ref, k_ref, vt_ref, kmean_ref, o_ref):
    ti = pl.program_id(2)
    lane_q = lax.broadcasted_iota(jnp.int32, (TQ, LANES), 1)
    heads = range(MOBA_GROUP)
    lane_h = lax.broadcasted_iota(jnp.int32, (QSUB, LANES), 1)
    blk = lax.broadcasted_iota(jnp.int32, (N_KV_BLOCKS, QSUB), 0)
    pad_rows = jnp.zeros((LANES - N_KV_BLOCKS, QSUB), F32)
    pair_lanes = lambda hh: slice((hh // 2) * LANES, (hh // 2 + 1) * LANES)
    qm = []
    for hh in heads:
        q2 = q_ref[:, pair_lanes(hh)]
        qm.append(jnp.where((lane_q // HEAD_DIM) == hh % 2, q2, jnp.zeros_like(q2)))
    q_plain = [jnp.concatenate(
        [qm[hh], jnp.broadcast_to(qaug_ref[hh:hh + 1, :], (TQ, LANES)).astype(BF16)], axis=1)
        for hh in heads]

    def select_lanes(t, hh):
        need_rank = NSUB * (t + 1) - 1 > MOBA_TOPK
        if need_rank:
            km = jnp.concatenate(
                [kmean_ref[:, pair_lanes(hh)], jnp.zeros((LANES - N_KV_BLOCKS, LANES), F32)],
                axis=0).astype(BF16)
            gt = _dot_nt(km, qm[hh])
        qx = []
        for d in range(NSUB):
            qb = NSUB * t + d
            if qb <= MOBA_TOPK:
                attend = blk <= qb
            else:
                g = gt[:N_KV_BLOCKS, d * QSUB:(d + 1) * QSUB]
                cnt = jnp.zeros((N_KV_BLOCKS, QSUB), F32)
                for mm in range(qb):
                    gm = g[mm:mm + 1, :]
                    beats = (gm > g) | ((gm == g) & (mm < blk))
                    cnt = cnt + jnp.where(beats, 1.0, 0.0)
                attend = ((blk < qb) & (cnt < MOBA_TOPK)) | (blk == qb)
            selb_t = jnp.where(attend, 0.0, NEG)
            selb = jnp.concatenate([selb_t, pad_rows], axis=0).T
            qx.append(jnp.where(lane_h < N_KV_BLOCKS, selb, qaug_ref[hh:hh + 1, :]).astype(BF16))
        return jnp.concatenate(qx, axis=0)

    def finish(t):
        q_list = [functools.partial(
            lambda hh: jnp.concatenate([qm[hh], select_lanes(t, hh)], axis=1), hh)
            for hh in heads]
        acc = _flash_chains(
            q_list,
            lambda c, st, size: k_ref[pl.ds(st, size), (c // 2) * 2 * LANES:(c // 2 + 1) * 2 * LANES],
            lambda c, st, size: vt_ref[c * MOBA_NV:(c + 1) * MOBA_NV, pl.ds(st, size)],
            t, q_first=q_plain)
        for pr in range(MOBA_GROUP // 2):
            out_t = jnp.concatenate(
                [a[:HEAD_DIM] / a[HEAD_DIM:HEAD_DIM + 1] for a in acc[2 * pr:2 * pr + 2]],
                axis=0)
            o_ref[:, pr * LANES:(pr + 1) * LANES] = out_t.T.astype(BF16)

    _per_query_tile(ti, finish)


def _moba_call(qaug, q, k, vt, kmean):
    nq = SEQ // TQ
    g = MOBA_GROUP
    return pl.pallas_call(
        _moba_kernel,
        grid=(BATCH, MOBA_HEADS // g, nq),
        in_specs=[
            pl.BlockSpec((None, 8, LANES), lambda b, h, i: (h, 0, 0)),
            pl.BlockSpec((TQ, g * HEAD_DIM), lambda b, h, i: (b * nq + i, h)),
            pl.BlockSpec((SEQ, g * LANES), lambda b, h, i: (b, h)),
            pl.BlockSpec((None, g * MOBA_NV, SEQ), lambda b, h, i: (b, h, 0)),
            pl.BlockSpec((None, N_KV_BLOCKS, g * HEAD_DIM), lambda b, h, i: (b, 0, h)),
        ],
        out_specs=pl.BlockSpec((TQ, g * HEAD_DIM), lambda b, h, i: (b * nq + i, h)),
        out_shape=jax.ShapeDtypeStruct((N_TOK, MOBA_W), BF16),
        compiler_params=pltpu.CompilerParams(
            dimension_semantics=("arbitrary", "arbitrary", "arbitrary"),
            vmem_limit_bytes=VMEM_LIMIT),
        name="moba",
    )(qaug, q, k, vt, kmean)


MLA_GROUP = 4


MLA_NV = MLA_V + ONES_ROWS


def _mla_kernel(q_ref, k_ref, vt_ref, o_ref):
    ti = pl.program_id(2)
    q_list = [q_ref[:, c * LANES:(c + 1) * LANES] for c in range(MLA_GROUP)]

    def finish(t):
        acc = _flash_chains(
            q_list,
            lambda c, st, size: k_ref[pl.ds(st, size), c * LANES:(c + 1) * LANES],
            lambda c, st, size: vt_ref[c * MLA_NV:(c + 1) * MLA_NV, pl.ds(st, size)],
            t)
        for pr in range(MLA_GROUP // 2):
            out_t = jnp.concatenate(
                [a[:MLA_V] / a[MLA_V:MLA_V + 1] for a in acc[2 * pr:2 * pr + 2]], axis=0)
            o_ref[:, pr * LANES:(pr + 1) * LANES] = out_t.T.astype(BF16)

    _per_query_tile(ti, finish)


def _mla_call(q, k, vt):
    nq = SEQ // TQ
    g = MLA_GROUP
    return pl.pallas_call(
        _mla_kernel,
        grid=(BATCH, MLA_HEADS // g, nq),
        in_specs=[
            pl.BlockSpec((TQ, g * LANES), lambda b, h, i: (b * nq + i, h)),
            pl.BlockSpec((SEQ, g * LANES), lambda b, h, i: (b, h)),
            pl.BlockSpec((None, g * MLA_NV, SEQ), lambda b, h, i: (b, h, 0)),
        ],
        out_specs=pl.BlockSpec((TQ, g * MLA_V), lambda b, h, i: (b * nq + i, h)),
        out_shape=jax.ShapeDtypeStruct((N_TOK, MLA_W), BF16),
        compiler_params=pltpu.CompilerParams(
            dimension_semantics=("arbitrary", "arbitrary", "arbitrary"),
            vmem_limit_bytes=VMEM_LIMIT),
        name="mla",
    )(q, k, vt)


DIFF_GROUP = 4


DIFF_NV = DIFF_V + ONES_ROWS


def _diff_kernel(lam_init, qaug_ref, q_ref, k_ref, vt_ref, lam_ref, g_ref, o_ref):
    ti = pl.program_id(2)
    lane_q = lax.broadcasted_iota(jnp.int32, (TQ, LANES), 1)
    q_list = []
    for hh in range(DIFF_GROUP):
        q2 = q_ref[:, hh * LANES:(hh + 1) * LANES]
        qx = jnp.broadcast_to(qaug_ref[hh:hh + 1, :], (TQ, LANES)).astype(BF16)
        for c in range(2):
            qm = jnp.where((lane_q // DIFF_QK) == c, q2, jnp.zeros_like(q2))
            q_list.append(jnp.concatenate([qm, qx], axis=1))

    lf = lam_ref[...]
    e1 = jnp.exp(jnp.sum(lf[0:1, :] * lf[1:2, :], axis=-1, keepdims=True))
    e2 = jnp.exp(jnp.sum(lf[2:3, :] * lf[3:4, :], axis=-1, keepdims=True))
    lam = e1 - e2 + lam_init

    def finish(t):
        acc = _flash_chains(
            q_list,
            lambda c, st, size: k_ref[pl.ds(st, size),
                                      (c // 2) * 2 * LANES:(c // 2 + 1) * 2 * LANES],
            lambda c, st, size: vt_ref[(c // 2) * DIFF_NV:(c // 2 + 1) * DIFF_NV,
                                       pl.ds(st, size)],
            t)
        for hh in range(DIFF_GROUP):
            a0, a1 = acc[2 * hh], acc[2 * hh + 1]
            o0 = a0[:DIFF_V] / a0[DIFF_V:DIFF_V + 1]
            o1 = a1[:DIFF_V] / a1[DIFF_V:DIFF_V + 1]
            y = (o0 - lam * o1).T
            y = y * lax.rsqrt(jnp.mean(y * y, axis=-1, keepdims=True) + 1e-5) * g_ref[...]
            o_ref[:, hh * LANES:(hh + 1) * LANES] = (y * (1.0 - lam_init)).astype(BF16)

    _per_query_tile(ti, finish)


def _diff_call(li, lam_init, qaug, q, k, vt, lam_p, subln_g):
    nq = SEQ // TQ
    g = DIFF_GROUP
    return pl.pallas_call(
        functools.partial(_diff_kernel, lam_init),
        grid=(BATCH, DIFF_HEADS // g, nq),
        in_specs=[
            pl.BlockSpec((None, 8, LANES), lambda b, h, i: (h, 0, 0)),
            pl.BlockSpec((TQ, g * LANES), lambda b, h, i: (b * nq + i, h)),
            pl.BlockSpec((SEQ, g * 2 * LANES), lambda b, h, i: (b, h)),
            pl.BlockSpec((None, g * DIFF_NV, SEQ), lambda b, h, i: (b, h, 0)),
            _layer_spec(lam_p, li),
            _layer_spec(subln_g, li),
        ],
        out_specs=pl.BlockSpec((TQ, g * LANES), lambda b, h, i: (b * nq + i, h)),
        out_shape=jax.ShapeDtypeStruct((N_TOK, DIFF_W), BF16),
        compiler_params=pltpu.CompilerParams(
            dimension_semantics=("arbitrary", "arbitrary", "arbitrary"),
            vmem_limit_bytes=VMEM_LIMIT),
        name="diff",
    )(qaug, q, k, vt, lam_p, subln_g)


def _merge_kernel(x_ref, ya_ref, yb_ref, yc_ref, za_ref, zb_ref, zc_ref, p_ref,
                  wm_ref, bm_ref, wa_ref, wb_ref, wc_ref, wo_ref, wpg_ref, wp_ref,
                  lng_ref, lnb_ref, o_ref):
    subs = [slice(s * TM_SUB, (s + 1) * TM_SUB) for s in range(TM_MERGE // TM_SUB)]
    branches = ((ya_ref, za_ref, wa_ref), (yb_ref, zb_ref, wb_ref), (yc_ref, zc_ref, wc_ref))

    def merge_branches(rows):
        xb = x_ref[rows, :].astype(BF16)
        merged = jnp.zeros((TM_SUB, D_MODEL), F32)
        for i, (y_ref, z_ref, w_ref) in enumerate(branches):
            sl = slice(i * D_MODEL, (i + 1) * D_MODEL)
            gate = jax.nn.sigmoid(_dot(xb, wm_ref[:, sl]) + bm_ref[:, sl])
            y = y_ref[rows, :] * z_ref[rows, :]
            merged = merged + gate * _dot(y, w_ref[...])
        return merged

    r = []
    for rows in subs:
        merged = merge_branches(rows)
        ple = _dot(p_ref[rows, :].astype(BF16), wp_ref[...])
        ri = ALPHA * x_ref[rows, :] + _dot(merged.astype(BF16), wo_ref[...])
        r.append(ri + jax.nn.sigmoid(_dot(ri.astype(BF16), wpg_ref[...])) * ple)
    for rows, ri in zip(subs, r):
        mu = jnp.mean(ri, axis=-1, keepdims=True)
        d = ri - mu
        var = jnp.mean(d * d, axis=-1, keepdims=True)
        o_ref[rows, :] = d * lax.rsqrt(var + NORM_EPS) * lng_ref[...] + lnb_ref[...]


def _merge_call(li, x2, ya, yb, yc, za, zb, zc, p2, wm, bm, wa, wb, wc, wo, wpg, wp, lng, lnb):
    n_tiles = N_TOK // TM_MERGE
    row = lambda w: pl.BlockSpec((TM_MERGE, w), lambda i: (i, 0))
    p_row = pl.BlockSpec((TM_MERGE, PLE_DIM), lambda i: (li * n_tiles + i, 0))
    consts = (wm, bm, wa, wb, wc, wo, wpg, wp, lng, lnb)
    return pl.pallas_call(
        _merge_kernel,
        grid=(n_tiles,),
        in_specs=[row(D_MODEL)] + [row(MOBA_W)] * 6 + [p_row]
                 + [_layer_spec(c, li) for c in consts],
        out_specs=row(D_MODEL),
        out_shape=jax.ShapeDtypeStruct((N_TOK, D_MODEL), F32),
        compiler_params=pltpu.CompilerParams(
            dimension_semantics=("arbitrary",), vmem_limit_bytes=VMEM_LIMIT),
        name="merge",
    )(x2, ya, yb, yc, za, zb, zc, p2, *consts)


def _prep_proj_weights(w_in, w_uq):
    pts = np.cumsum([MOBA_W] * 4 + [MLA_Q_LORA, MLA_KV_LORA, MLA_ROPE, MLA_W] + [DIFF_W] * 4)
    w_bf = w_in.astype(BF16)
    kr = w_in[..., pts[5]:pts[6]]
    z_lo = jnp.zeros(w_in.shape[:-1] + (MLA_NOPE,), F32)
    wkr = jnp.concatenate([z_lo, kr, _rot_half_cols(kr)], axis=-1)
    wcat = (w_bf, wkr.astype(BF16), w_bf[..., pts[6]:])

    dq = MLA_NOPE + MLA_ROPE
    uq = w_uq.reshape(DEPTH, MLA_Q_LORA, MLA_HEADS, dq)
    pad_q = jnp.zeros((DEPTH, MLA_Q_LORA, MLA_HEADS, LANES - dq), F32)
    wq = jnp.concatenate([uq, pad_q], axis=-1)
    wqs = _rot_half_cols(uq[..., MLA_NOPE:])
    flat = lambda w: w.reshape(DEPTH, MLA_Q_LORA, -1).astype(BF16)
    return wcat, flat(wq), flat(wqs)


def kernel(x, p, w_in, mla_q_norm_g, mla_kv_norm_g, mla_w_uq, mla_w_ukv, diff_lambda, diff_subln_g, w_branch_a, w_branch_b, w_branch_c, w_merge, b_merge, w_out, ln_g, ln_b, w_ple_gate, w_ple):
    slopes_a, slopes_c = _alibi_slopes()
    qaug_a = _query_aug_rows(slopes_a, MOBA_GROUP)
    qaug_c = _query_aug_rows(slopes_c, DIFF_GROUP)
    ktab = _key_aug_table()
    cos_t, sin_t = _rope_tables()
    bf = lambda w: w.astype(BF16)
    vec = lambda v: v[:, None, :]
    wcat, wq, wqs = _prep_proj_weights(w_in, mla_w_uq)
    wkv = bf(mla_w_ukv)
    merge_consts = (bf(w_merge), vec(b_merge), bf(w_branch_a), bf(w_branch_b), bf(w_branch_c),
                    bf(w_out), bf(w_ple_gate), bf(w_ple), vec(ln_g), vec(ln_b))
    gq, gkv, subln_g = vec(mla_q_norm_g), vec(mla_kv_norm_g), vec(diff_subln_g)
    p2 = p.reshape(DEPTH * N_TOK, PLE_DIM)
    h = x.reshape(N_TOK, D_MODEL)
    for i in range(DEPTH):
        (aq, ak, avt, az, kmean, bq, bk, bvt, bz, cq, ck, cvt, cz) = _proj_call(
            i, h, wcat, wq, wqs, wkv, gq, gkv, cos_t, sin_t, ktab)
        kmean = kmean.reshape(BATCH, N_KV_BLOCKS, MOBA_W)
        ya = _moba_call(qaug_a, aq, ak, avt, kmean)
        yb = _mla_call(bq, bk, bvt)
        lam_init = 0.8 - 0.6 * math.exp(-0.3 * i)
        yc = _diff_call(i, lam_init, qaug_c, cq, ck, cvt, diff_lambda, subln_g)
        h = _merge_call(i, h, ya, yb, yc, az, bz, cz, p2, *merge_consts)
    return h.reshape(BATCH, SEQ, D_MODEL)
```

```python
import functools
import math

import numpy as np
import jax
import jax.numpy as jnp
from jax import lax
from jax.experimental import pallas as pl
from jax.experimental.pallas import tpu as pltpu

D_MODEL = 1024
BATCH = 8
SEQ = 2048
DEPTH = 2
MOBA_HEADS = 8
HEAD_DIM = 64
MOBA_BLOCK = 256
MOBA_TOPK = 3
MLA_HEADS = 8
MLA_Q_LORA = 384
MLA_KV_LORA = 256
MLA_NOPE = 64
MLA_ROPE = 32
MLA_V = 64
ROPE_THETA = 10000.0
DIFF_HEADS = 4
DIFF_QK = 64
DIFF_V = 2 * DIFF_QK
PLE_DIM = 256
NORM_EPS = 1e-5
NEG = -1e30
ALPHA = (2 * DEPTH) ** 0.25
LOG2E = math.log2(math.e)

MOBA_W = MOBA_HEADS * HEAD_DIM
MLA_W = MLA_HEADS * MLA_V
DIFF_W = DIFF_HEADS * DIFF_V
N_TOK = BATCH * SEQ
N_KV_BLOCKS = SEQ // MOBA_BLOCK

LANES = 128
VMEM_LIMIT = 54 * 1024 * 1024
TM = 512
TM_MERGE = 512
TM_SUB = MOBA_BLOCK
TQ = 1024
TK = 512
QSUB = MOBA_BLOCK
NSUB = TQ // QSUB

OFF_A = 0
OFF_CQ = OFF_A + 4 * MOBA_W
OFF_CKV = OFF_CQ + MLA_Q_LORA
OFF_KR = OFF_CKV + MLA_KV_LORA
OFF_BZ = OFF_KR + LANES
OFF_C = OFF_BZ + MLA_W

F32 = jnp.float32
BF16 = jnp.bfloat16


def _dot(a, b):
    return jnp.dot(a, b, preferred_element_type=F32)


def _dot_nt(a, b):
    return lax.dot_general(a, b, (((1,), (1,)), ((), ())), preferred_element_type=F32)


def _alibi_slopes():
    n = MOBA_HEADS + DIFF_HEADS
    s = 2.0 ** (-8.0 * (np.arange(n) + 1) / n)
    diff_idx = np.arange(DIFF_HEADS) * (n // DIFF_HEADS)
    moba_idx = np.setdiff1d(np.arange(n), diff_idx)
    return (jnp.asarray(s[moba_idx], dtype=F32), jnp.asarray(s[diff_idx], dtype=F32))


def _rope_tables():
    d = MLA_ROPE
    freqs = ROPE_THETA ** (-np.arange(0, d, 2, dtype=np.float64) / d)
    ang = np.arange(SEQ, dtype=np.float64)[:, None] * freqs[None, :]
    cos, sin = np.cos(ang), np.sin(ang)
    cos_t = np.zeros((SEQ, LANES), np.float32)
    sin_t = np.zeros((SEQ, LANES), np.float32)
    cos_t[:, :MLA_NOPE] = 1.0
    cos_t[:, MLA_NOPE:MLA_NOPE + d] = np.concatenate([cos, cos], axis=-1)
    sin_t[:, MLA_NOPE:MLA_NOPE + d] = np.concatenate([sin, sin], axis=-1)
    return jnp.asarray(cos_t), jnp.asarray(sin_t)


def _rot_half_cols(w):
    half = MLA_ROPE // 2
    return jnp.concatenate([-w[..., half:], w[..., :half]], axis=-1)


def _proj_kernel(x_ref, whead_ref, wkr_ref, wtail_ref, wq_ref, wqs_ref, wkv_ref, gq_ref, gkv_ref,
                 cos_ref, sin_ref, ktab_ref,
                 aq_ref, ak_ref, avt_ref, az_ref, kmean_ref,
                 bq_ref, bk_ref, bvt_ref, bz_ref,
                 cq_ref, ck_ref, cvt_ref, cz_ref):
    for st in range(TM // TM_SUB):
        _proj_sub_tile(st, x_ref, whead_ref, wkr_ref, wtail_ref, wq_ref, wqs_ref, wkv_ref,
                       gq_ref, gkv_ref, cos_ref, sin_ref, ktab_ref,
                       aq_ref, ak_ref, avt_ref, az_ref, kmean_ref,
                       bq_ref, bk_ref, bvt_ref, bz_ref, cq_ref, ck_ref, cvt_ref, cz_ref)


def _proj_sub_tile(st, x_ref, whead_ref, wkr_ref, wtail_ref, wq_ref, wqs_ref, wkv_ref,
                   gq_ref, gkv_ref, cos_ref, sin_ref, ktab_ref,
                   aq_ref, ak_ref, avt_ref, az_ref, kmean_ref,
                   bq_ref, bk_ref, bvt_ref, bz_ref, cq_ref, ck_ref, cvt_ref, cz_ref):
    rows = slice(st * TM_SUB, (st + 1) * TM_SUB)
    xb = x_ref[rows, :].astype(BF16)
    ktab = ktab_ref[rows, :]
    ones_rows = jnp.where(lax.broadcasted_iota(jnp.int32, (ONES_ROWS, TM_SUB), 0) == 0,
                          1.0, 0.0).astype(BF16)

    def store_vt(vt_ref, v, dv, stride=None, offset=0):
        stride = stride or dv
        vt = v.T.astype(BF16)
        for h in range(v.shape[1] // stride):
            base = h * (dv + ONES_ROWS)
            vt_ref[base:base + dv, rows] = vt[h * stride + offset:h * stride + offset + dv]
            vt_ref[base + dv:base + dv + ONES_ROWS, rows] = ones_rows

    def store_k_aug(k_ref, k):
        for h in range(k.shape[1] // LANES):
            k_ref[rows, 2 * h * LANES:(2 * h + 1) * LANES] = (
                k[:, h * LANES:(h + 1) * LANES].astype(BF16))
            k_ref[rows, (2 * h + 1) * LANES:(2 * h + 2) * LANES] = ktab

    def silu(z):
        return z * jax.nn.sigmoid(z)

    def proj(off, width):
        if off < OFF_KR:
            return _dot(xb, whead_ref[:, off:off + width])
        if off == OFF_KR:
            return _dot(xb, wkr_ref[...])
        return _dot(xb, wtail_ref[:, off - OFF_BZ:off - OFF_BZ + width])

    cos_t = cos_ref[rows, :]
    sin_t = sin_ref[rows, :]
    cq = proj(OFF_CQ, MLA_Q_LORA)
    ckv = proj(OFF_CKV, MLA_KV_LORA)
    kr = proj(OFF_KR, LANES)
    krot = kr * cos_t + pltpu.roll(kr, LANES - MLA_ROPE, 1) * sin_t
    cqn = cq * lax.rsqrt(jnp.mean(cq * cq, axis=-1, keepdims=True) + 1e-6) * gq_ref[...]
    cqn = cqn.astype(BF16)
    ckvn = ckv * lax.rsqrt(jnp.mean(ckv * ckv, axis=-1, keepdims=True) + 1e-6) * gkv_ref[...]
    ckvn = ckvn.astype(BF16)

    aq_ref[rows, :] = (proj(OFF_A, MOBA_W) * (HEAD_DIM ** -0.5 * LOG2E)).astype(BF16)
    ka = proj(OFF_A + MOBA_W, MOBA_W)
    store_k_aug(ak_ref, ka)
    kmean_ref[st] = jnp.mean(ka, axis=0, keepdims=True)
    store_vt(avt_ref, proj(OFF_A + 2 * MOBA_W, MOBA_W), HEAD_DIM)
    az_ref[rows, :] = silu(proj(OFF_A + 3 * MOBA_W, MOBA_W)).astype(BF16)

    qa = _dot(cqn, wq_ref[...])
    qb = _dot(cqn, wqs_ref[...])
    qscale = (MLA_NOPE + MLA_ROPE) ** -0.5 * LOG2E
    cos_q = cos_t * qscale
    sin_q = sin_t * qscale
    heads_per_tile = LANES // MLA_ROPE
    for h in range(MLA_HEADS):
        sl = slice(h * LANES, (h + 1) * LANES)
        qb_tile = qb[:, (h // heads_per_tile) * LANES:(h // heads_per_tile + 1) * LANES]
        shift = (MLA_NOPE - (h % heads_per_tile) * MLA_ROPE) % LANES
        partner = pltpu.roll(qb_tile, shift, 1) if shift else qb_tile
        bq_ref[rows, sl] = (qa[:, sl] * cos_q + partner * sin_q).astype(BF16)
    kv = _dot(ckvn, wkv_ref[...])
    nope_lanes = lax.broadcasted_iota(jnp.int32, (TM_SUB, LANES), 1) < MLA_NOPE
    for h in range(MLA_HEADS):
        sl = slice(h * LANES, (h + 1) * LANES)
        bk_ref[rows, sl] = jnp.where(nope_lanes, kv[:, sl], krot).astype(BF16)
    store_vt(bvt_ref, kv, MLA_V, stride=MLA_NOPE + MLA_V, offset=MLA_NOPE)
    bz_ref[rows, :] = silu(proj(OFF_BZ, MLA_W)).astype(BF16)

    cq_ref[rows, :] = (proj(OFF_C, DIFF_W) * (DIFF_QK ** -0.5 * LOG2E)).astype(BF16)
    store_k_aug(ck_ref, proj(OFF_C + DIFF_W, DIFF_W))
    store_vt(cvt_ref, proj(OFF_C + 2 * DIFF_W, DIFF_W), DIFF_V)
    cz_ref[rows, :] = silu(proj(OFF_C + 3 * DIFF_W, DIFF_W)).astype(BF16)


def _layer_spec(stacked, li):
    zeros = (0,) * (stacked.ndim - 1)
    return pl.BlockSpec((None,) + stacked.shape[1:], lambda *_: (li,) + zeros,
                        pipeline_mode=pl.Buffered(1))


def _proj_call(li, x2, wcat, wq, wqs, wkv, gq, gkv, cos_t, sin_t, ktab):
    n_tiles = N_TOK // TM
    pos_tiles = SEQ // TM
    row = lambda w: pl.BlockSpec((TM, w), lambda i: (i, 0))
    tab = pl.BlockSpec((TM, LANES), lambda i: (i % pos_tiles, 0))
    out_shapes = []
    out_specs = []

    def add(width):
        out_shapes.append(jax.ShapeDtypeStruct((N_TOK, width), BF16))
        out_specs.append(row(width))

    def add_vt(heads, dv):
        rows = heads * (dv + ONES_ROWS)
        out_shapes.append(jax.ShapeDtypeStruct((BATCH, rows, SEQ), BF16))
        out_specs.append(pl.BlockSpec((None, rows, TM),
                                      lambda i: (i // pos_tiles, 0, i % pos_tiles)))

    add(MOBA_W); add(2 * MOBA_W); add_vt(MOBA_HEADS, HEAD_DIM); add(MOBA_W)
    subs = TM // TM_SUB
    out_shapes.append(jax.ShapeDtypeStruct((n_tiles * subs, 1, MOBA_W), F32))
    out_specs.append(pl.BlockSpec((subs, 1, MOBA_W), lambda i: (i, 0, 0)))
    add(MLA_HEADS * LANES); add(MLA_HEADS * LANES); add_vt(MLA_HEADS, MLA_V); add(MLA_W)
    add(DIFF_W); add(2 * DIFF_W); add_vt(DIFF_HEADS, DIFF_V); add(DIFF_W)
    return pl.pallas_call(
        _proj_kernel,
        grid=(n_tiles,),
        in_specs=[row(D_MODEL),
                  pl.BlockSpec((None, D_MODEL, OFF_KR), lambda i: (li, 0, 0),
                               pipeline_mode=pl.Buffered(1))]
                 + [_layer_spec(w, li) for w in (*wcat[1:], wq, wqs, wkv, gq, gkv)]
                 + [tab, tab, tab],
        out_specs=out_specs,
        out_shape=out_shapes,
        compiler_params=pltpu.CompilerParams(
            dimension_semantics=("arbitrary",), vmem_limit_bytes=VMEM_LIMIT),
        name="proj",
    )(x2, *wcat, wq, wqs, wkv, gq, gkv, cos_t, sin_t, ktab)


AUG_POS = 8
POS_RADIX = 256
ONES_ROWS = 16
SUBLANES = 8


def _split3(v):
    hi = v.astype(BF16).astype(F32)
    mid = (v - hi).astype(BF16).astype(F32)
    lo = (v - hi - mid).astype(BF16).astype(F32)
    return hi, mid, lo


def _key_aug_table():
    kpos = np.arange(SEQ)
    t = np.zeros((SEQ, LANES), np.float32)
    t[kpos, kpos // MOBA_BLOCK] = 1.0
    t[:, AUG_POS:AUG_POS + 3] = (kpos // POS_RADIX)[:, None]
    t[:, AUG_POS + 3:AUG_POS + 6] = (kpos % POS_RADIX)[:, None]
    return jnp.asarray(t).astype(BF16)


def _query_aug_rows(slopes, group):
    s2 = slopes * LOG2E
    pieces = jnp.stack(_split3(s2 * POS_RADIX) + _split3(s2), axis=-1)
    rows = jnp.zeros((slopes.shape[0], LANES), F32).at[:, AUG_POS:AUG_POS + 6].set(pieces)
    rows = rows.reshape(slopes.shape[0] // group, group, LANES)
    return jnp.pad(rows, ((0, 0), (0, SUBLANES - group), (0, 0)))


def _flash_chains(q_list, k_tile, vt_tile, ti):
    n = len(q_list)
    nsc = n * NSUB
    key_i = lax.broadcasted_iota(jnp.int32, (TQ, QSUB), 0)
    qry_i = lax.broadcasted_iota(jnp.int32, (TQ, QSUB), 1)

    q_sub = [q_list[c][d * QSUB:(d + 1) * QSUB] for c in range(n) for d in range(NSUB)]
    masks = [(key_i <= qry_i + d * QSUB)[:(d + 1) * QSUB] for d in range(NSUB)]
    items = [(sc, ti * TQ, (sc % NSUB + 1) * QSUB, masks[sc % NSUB]) for sc in range(nsc)]
    for j in range(ti * TQ // TK):
        items += [(sc, j * TK, TK, None) for sc in range(nsc)]

    m = [None] * nsc
    acc = [None] * nsc
    live = {}

    def scores(i):
        sc, start, size, mask = items[i]
        s = _dot_nt(k_tile(sc // NSUB, start, size), q_sub[sc])
        if mask is not None:
            s = jnp.where(mask, s, NEG)
        smax = jnp.max(s, axis=0, keepdims=True)
        m_new = smax if m[sc] is None else jnp.maximum(m[sc], smax)
        a = None if m[sc] is None else jnp.exp2(m[sc] - m_new)
        m[sc] = m_new
        live[i] = (s, m_new, a)

    def probs(i):
        s, m_new, a = live[i]
        live[i] = (jnp.exp2(s - m_new).astype(BF16), a)

    def values(i):
        sc, start, size, _ = items[i]
        p, a = live.pop(i)
        pv = _dot(vt_tile(sc // NSUB, start, size), p)
        acc[sc] = pv if a is None else a * acc[sc] + pv

    for r in range(0, len(items), nsc):
        rnd = range(r, r + nsc)
        for i in rnd:
            scores(i)
        for i in rnd:
            probs(i)
            values(i)
    return [jnp.concatenate(acc[c * NSUB:(c + 1) * NSUB], axis=1) for c in range(n)]


def _per_query_tile(ti, fn):
    for t in range(SEQ // TQ):
        pl.when(ti == t)(functools.partial(fn, t))


MOBA_NV = HEAD_DIM + ONES_ROWS
MOBA_GROUP = 4


def _moba_kernel(qaug_ref, q_ref, k_ref, vt_ref, kmean_ref, o_ref):
    ti = pl.program_id(2)
    lane_q = lax.broadcasted_iota(jnp.int32, (TQ, LANES), 1)
    heads = range(MOBA_GROUP)
    lane_h = lax.broadcasted_iota(jnp.int32, (QSUB, LANES), 1)
    blk = lax.broadcasted_iota(jnp.int32, (N_KV_BLOCKS, QSUB), 0)
    pad_rows = jnp.zeros((LANES - N_KV_BLOCKS, QSUB), F32)
    pair_lanes = lambda hh: slice((hh // 2) * LANES, (hh // 2 + 1) * LANES)
    qm = []
    for hh in heads:
        q2 = q_ref[:, pair_lanes(hh)]
        qm.append(jnp.where((lane_q // HEAD_DIM) == hh % 2, q2, jnp.zeros_like(q2)))

    def select_lanes(t, hh):
        need_rank = NSUB * (t + 1) - 1 > MOBA_TOPK
        if need_rank:
            km = jnp.concatenate(
                [kmean_ref[:, pair_lanes(hh)], jnp.zeros((LANES - N_KV_BLOCKS, LANES), F32)],
                axis=0).astype(BF16)
            gt = _dot_nt(km, qm[hh])
        qx = []
        for d in range(NSUB):
            qb = NSUB * t + d
            if qb <= MOBA_TOPK:
                attend = blk <= qb
            else:
                g = gt[:N_KV_BLOCKS, d * QSUB:(d + 1) * QSUB]
                cnt = jnp.zeros((N_KV_BLOCKS, QSUB), F32)
                for mm in range(qb):
                    gm = g[mm:mm + 1, :]
                    beats = (gm > g) | ((gm == g) & (mm < blk))
                    cnt = cnt + jnp.where(beats, 1.0, 0.0)
                attend = ((blk < qb) & (cnt < MOBA_TOPK)) | (blk == qb)
            selb_t = jnp.where(attend, 0.0, NEG)
            selb = jnp.concatenate([selb_t, pad_rows], axis=0).T
            qx.append(jnp.where(lane_h < N_KV_BLOCKS, selb, qaug_ref[hh:hh + 1, :]).astype(BF16))
        return jnp.concatenate(qx, axis=0)

    def finish(t):
        q_list = [jnp.concatenate([qm[hh], select_lanes(t, hh)], axis=1) for hh in heads]
        acc = _flash_chains(
            q_list,
            lambda c, st, size: k_ref[pl.ds(st, size), (c // 2) * 2 * LANES:(c // 2 + 1) * 2 * LANES],
            lambda c, st, size: vt_ref[c * MOBA_NV:(c + 1) * MOBA_NV, pl.ds(st, size)],
            t)
        for pr in range(MOBA_GROUP // 2):
            out_t = jnp.concatenate(
                [a[:HEAD_DIM] / a[HEAD_DIM:HEAD_DIM + 1] for a in acc[2 * pr:2 * pr + 2]],
                axis=0)
            o_ref[:, pr * LANES:(pr + 1) * LANES] = out_t.T.astype(BF16)

    _per_query_tile(ti, finish)


def _moba_call(qaug, q, k, vt, kmean):
    nq = SEQ // TQ
    g = MOBA_GROUP
    return pl.pallas_call(
        _moba_kernel,
        grid=(BATCH, MOBA_HEADS // g, nq),
        in_specs=[
            pl.BlockSpec((None, SUBLANES, LANES), lambda b, h, i: (h, 0, 0)),
            pl.BlockSpec((TQ, g * HEAD_DIM), lambda b, h, i: (b * nq + i, h)),
            pl.BlockSpec((SEQ, g * LANES), lambda b, h, i: (b, h)),
            pl.BlockSpec((None, g * MOBA_NV, SEQ), lambda b, h, i: (b, h, 0)),
            pl.BlockSpec((None, N_KV_BLOCKS, g * HEAD_DIM), lambda b, h, i: (b, 0, h)),
        ],
        out_specs=pl.BlockSpec((TQ, g * HEAD_DIM), lambda b, h, i: (b * nq + i, h)),
        out_shape=jax.ShapeDtypeStruct((N_TOK, MOBA_W), BF16),
        compiler_params=pltpu.CompilerParams(
            dimension_semantics=("arbitrary", "arbitrary", "arbitrary"),
            vmem_limit_bytes=VMEM_LIMIT),
        name="moba",
    )(qaug, q, k, vt, kmean)


MLA_GROUP = 4


MLA_NV = MLA_V + ONES_ROWS


def _mla_kernel(q_ref, k_ref, vt_ref, o_ref):
    ti = pl.program_id(2)
    q_list = [q_ref[:, c * LANES:(c + 1) * LANES] for c in range(MLA_GROUP)]

    def finish(t):
        acc = _flash_chains(
            q_list,
            lambda c, st, size: k_ref[pl.ds(st, size), c * LANES:(c + 1) * LANES],
            lambda c, st, size: vt_ref[c * MLA_NV:(c + 1) * MLA_NV, pl.ds(st, size)],
            t)
        for pr in range(MLA_GROUP // 2):
            out_t = jnp.concatenate(
                [a[:MLA_V] / a[MLA_V:MLA_V + 1] for a in acc[2 * pr:2 * pr + 2]], axis=0)
            o_ref[:, pr * LANES:(pr + 1) * LANES] = out_t.T.astype(BF16)

    _per_query_tile(ti, finish)


def _mla_call(q, k, vt):
    nq = SEQ // TQ
    g = MLA_GROUP
    return pl.pallas_call(
        _mla_kernel,
        grid=(BATCH, MLA_HEADS // g, nq),
        in_specs=[
            pl.BlockSpec((TQ, g * LANES), lambda b, h, i: (b * nq + i, h)),
            pl.BlockSpec((SEQ, g * LANES), lambda b, h, i: (b, h)),
            pl.BlockSpec((None, g * MLA_NV, SEQ), lambda b, h, i: (b, h, 0)),
        ],
        out_specs=pl.BlockSpec((TQ, g * MLA_V), lambda b, h, i: (b * nq + i, h)),
        out_shape=jax.ShapeDtypeStruct((N_TOK, MLA_W), BF16),
        compiler_params=pltpu.CompilerParams(
            dimension_semantics=("arbitrary", "arbitrary", "arbitrary"),
            vmem_limit_bytes=VMEM_LIMIT),
        name="mla",
    )(q, k, vt)


DIFF_GROUP = 4


DIFF_NV = DIFF_V + ONES_ROWS


def _diff_kernel(lam_init, qaug_ref, q_ref, k_ref, vt_ref, lam_ref, g_ref, o_ref):
    ti = pl.program_id(2)
    lane_q = lax.broadcasted_iota(jnp.int32, (TQ, LANES), 1)
    q_list = []
    for hh in range(DIFF_GROUP):
        q2 = q_ref[:, hh * LANES:(hh + 1) * LANES]
        qx = jnp.broadcast_to(qaug_ref[hh:hh + 1, :], (TQ, LANES)).astype(BF16)
        for c in range(2):
            qm = jnp.where((lane_q // DIFF_QK) == c, q2, jnp.zeros_like(q2))
            q_list.append(jnp.concatenate([qm, qx], axis=1))

    lf = lam_ref[...]
    e1 = jnp.exp(jnp.sum(lf[0:1, :] * lf[1:2, :], axis=-1, keepdims=True))
    e2 = jnp.exp(jnp.sum(lf[2:3, :] * lf[3:4, :], axis=-1, keepdims=True))
    lam = e1 - e2 + lam_init

    def finish(t):
        acc = _flash_chains(
            q_list,
            lambda c, st, size: k_ref[pl.ds(st, size),
                                      (c // 2) * 2 * LANES:(c // 2 + 1) * 2 * LANES],
            lambda c, st, size: vt_ref[(c // 2) * DIFF_NV:(c // 2 + 1) * DIFF_NV,
                                       pl.ds(st, size)],
            t)
        for hh in range(DIFF_GROUP):
            a0, a1 = acc[2 * hh], acc[2 * hh + 1]
            o0 = a0[:DIFF_V] / a0[DIFF_V:DIFF_V + 1]
            o1 = a1[:DIFF_V] / a1[DIFF_V:DIFF_V + 1]
            y = (o0 - lam * o1).T
            y = y * lax.rsqrt(jnp.mean(y * y, axis=-1, keepdims=True) + 1e-5) * g_ref[...]
            o_ref[:, hh * LANES:(hh + 1) * LANES] = (y * (1.0 - lam_init)).astype(BF16)

    _per_query_tile(ti, finish)


def _diff_call(li, lam_init, qaug, q, k, vt, lam_p, subln_g):
    nq = SEQ // TQ
    g = DIFF_GROUP
    return pl.pallas_call(
        functools.partial(_diff_kernel, lam_init),
        grid=(BATCH, DIFF_HEADS // g, nq),
        in_specs=[
            pl.BlockSpec((None, SUBLANES, LANES), lambda b, h, i: (h, 0, 0)),
            pl.BlockSpec((TQ, g * LANES), lambda b, h, i: (b * nq + i, h)),
            pl.BlockSpec((SEQ, g * 2 * LANES), lambda b, h, i: (b, h)),
            pl.BlockSpec((None, g * DIFF_NV, SEQ), lambda b, h, i: (b, h, 0)),
            _layer_spec(lam_p, li),
            _layer_spec(subln_g, li),
        ],
        out_specs=pl.BlockSpec((TQ, g * LANES), lambda b, h, i: (b * nq + i, h)),
        out_shape=jax.ShapeDtypeStruct((N_TOK, DIFF_W), BF16),
        compiler_params=pltpu.CompilerParams(
            dimension_semantics=("arbitrary", "arbitrary", "arbitrary"),
            vmem_limit_bytes=VMEM_LIMIT),
        name="diff",
    )(qaug, q, k, vt, lam_p, subln_g)


def _merge_kernel(x_ref, ya_ref, yb_ref, yc_ref, za_ref, zb_ref, zc_ref, p_ref,
                  wm_ref, bm_ref, wa_ref, wb_ref, wc_ref, wo_ref, wpg_ref, wp_ref,
                  lng_ref, lnb_ref, o_ref):
    subs = [slice(s * TM_SUB, (s + 1) * TM_SUB) for s in range(TM_MERGE // TM_SUB)]
    branches = ((ya_ref, za_ref, wa_ref), (yb_ref, zb_ref, wb_ref), (yc_ref, zc_ref, wc_ref))

    def merge_branches(rows):
        xb = x_ref[rows, :].astype(BF16)
        merged = jnp.zeros((TM_SUB, D_MODEL), F32)
        for i, (y_ref, z_ref, w_ref) in enumerate(branches):
            sl = slice(i * D_MODEL, (i + 1) * D_MODEL)
            gate = jax.nn.sigmoid(_dot(xb, wm_ref[:, sl]) + bm_ref[:, sl])
            y = y_ref[rows, :] * z_ref[rows, :]
            merged = merged + gate * _dot(y, w_ref[...])
        return merged

    r = []
    for rows in subs:
        merged = merge_branches(rows)
        ple = _dot(p_ref[rows, :].astype(BF16), wp_ref[...])
        ri = ALPHA * x_ref[rows, :] + _dot(merged.astype(BF16), wo_ref[...])
        r.append(ri + jax.nn.sigmoid(_dot(ri.astype(BF16), wpg_ref[...])) * ple)
    for rows, ri in zip(subs, r):
        mu = jnp.mean(ri, axis=-1, keepdims=True)
        d = ri - mu
        var = jnp.mean(d * d, axis=-1, keepdims=True)
        o_ref[rows, :] = d * lax.rsqrt(var + NORM_EPS) * lng_ref[...] + lnb_ref[...]


def _merge_call(li, x2, ya, yb, yc, za, zb, zc, p2, wm, bm, wa, wb, wc, wo, wpg, wp, lng, lnb):
    n_tiles = N_TOK // TM_MERGE
    row = lambda w: pl.BlockSpec((TM_MERGE, w), lambda i: (i, 0))
    p_row = pl.BlockSpec((TM_MERGE, PLE_DIM), lambda i: (li * n_tiles + i, 0))
    consts = (wm, bm, wa, wb, wc, wo, wpg, wp, lng, lnb)
    return pl.pallas_call(
        _merge_kernel,
        grid=(n_tiles,),
        in_specs=[row(D_MODEL)] + [row(MOBA_W)] * 6 + [p_row]
                 + [_layer_spec(c, li) for c in consts],
        out_specs=row(D_MODEL),
        out_shape=jax.ShapeDtypeStruct((N_TOK, D_MODEL), F32),
        compiler_params=pltpu.CompilerParams(
            dimension_semantics=("arbitrary",), vmem_limit_bytes=VMEM_LIMIT),
        name="merge",
    )(x2, ya, yb, yc, za, zb, zc, p2, *consts)


def _prep_proj_weights(w_in, w_uq):
    pts = np.cumsum([MOBA_W] * 4 + [MLA_Q_LORA, MLA_KV_LORA, MLA_ROPE, MLA_W] + [DIFF_W] * 4)
    w_bf = w_in.astype(BF16)
    kr = w_in[..., pts[5]:pts[6]]
    z_lo = jnp.zeros(w_in.shape[:-1] + (MLA_NOPE,), F32)
    wkr = jnp.concatenate([z_lo, kr, _rot_half_cols(kr)], axis=-1)
    wcat = (w_bf, wkr.astype(BF16), w_bf[..., pts[6]:])

    dq = MLA_NOPE + MLA_ROPE
    uq = w_uq.reshape(DEPTH, MLA_Q_LORA, MLA_HEADS, dq)
    pad_q = jnp.zeros((DEPTH, MLA_Q_LORA, MLA_HEADS, LANES - dq), F32)
    wq = jnp.concatenate([uq, pad_q], axis=-1)
    wqs = _rot_half_cols(uq[..., MLA_NOPE:])
    flat = lambda w: w.reshape(DEPTH, MLA_Q_LORA, -1).astype(BF16)
    return wcat, flat(wq), flat(wqs)


def kernel(x, p, w_in, mla_q_norm_g, mla_kv_norm_g, mla_w_uq, mla_w_ukv, diff_lambda, diff_subln_g, w_branch_a, w_branch_b, w_branch_c, w_merge, b_merge, w_out, ln_g, ln_b, w_ple_gate, w_ple):
    slopes_a, slopes_c = _alibi_slopes()
    qaug_a = _query_aug_rows(slopes_a, MOBA_GROUP)
    qaug_c = _query_aug_rows(slopes_c, DIFF_GROUP)
    ktab = _key_aug_table()
    cos_t, sin_t = _rope_tables()
    bf = lambda w: w.astype(BF16)
    vec = lambda v: v[:, None, :]
    wcat, wq, wqs = _prep_proj_weights(w_in, mla_w_uq)
    wkv = bf(mla_w_ukv)
    merge_consts = (bf(w_merge), vec(b_merge), bf(w_branch_a), bf(w_branch_b), bf(w_branch_c),
                    bf(w_out), bf(w_ple_gate), bf(w_ple), vec(ln_g), vec(ln_b))
    gq, gkv, subln_g = vec(mla_q_norm_g), vec(mla_kv_norm_g), vec(diff_subln_g)
    p2 = p.reshape(DEPTH * N_TOK, PLE_DIM)
    h = x.reshape(N_TOK, D_MODEL)
    for i in range(DEPTH):
        (aq, ak, avt, az, kmean, bq, bk, bvt, bz, cq, ck, cvt, cz) = _proj_call(
            i, h, wcat, wq, wqs, wkv, gq, gkv, cos_t, sin_t, ktab)
        kmean = kmean.reshape(BATCH, N_KV_BLOCKS, MOBA_W)
        ya = _moba_call(qaug_a, aq, ak, avt, kmean)
        yb = _mla_call(bq, bk, bvt)
        lam_init = 0.8 - 0.6 * math.exp(-0.3 * i)
        yc = _diff_call(i, lam_init, qaug_c, cq, ck, cvt, diff_lambda, subln_g)
        h = _merge_call(i, h, ya, yb, yc, az, bz, cz, p2, *merge_consts)
    return h.reshape(BATCH, SEQ, D_MODEL)
```

```python
import functools
import math

import numpy as np
import jax
import jax.numpy as jnp
from jax import lax
from jax.experimental import pallas as pl
from jax.experimental.pallas import tpu as pltpu

D_MODEL = 1024
BATCH = 8
SEQ = 2048
DEPTH = 2
MOBA_HEADS = 8
HEAD_DIM = 64
MOBA_BLOCK = 256
MOBA_TOPK = 3
MLA_HEADS = 8
MLA_Q_LORA = 384
MLA_KV_LORA = 256
MLA_NOPE = 64
MLA_ROPE = 32
MLA_V = 64
ROPE_THETA = 10000.0
DIFF_HEADS = 4
DIFF_QK = 64
DIFF_V = 2 * DIFF_QK
PLE_DIM = 256
NORM_EPS = 1e-5
NEG = -1e30
N_BRANCH = 3
ALPHA = (2 * DEPTH) ** 0.25
LOG2E = math.log2(math.e)

MOBA_W = MOBA_HEADS * HEAD_DIM
MLA_W = MLA_HEADS * MLA_V
DIFF_W = DIFF_HEADS * DIFF_V
N_TOK = BATCH * SEQ
N_KV_BLOCKS = SEQ // MOBA_BLOCK

LANES = 128
VMEM_LIMIT = 54 * 1024 * 1024
TM = 512
TM_MERGE = 512
TM_SUB = MOBA_BLOCK
TQ = 1024
TK = 512
QSUB = MOBA_BLOCK
NSUB = TQ // QSUB

OFF_A = 0
OFF_CQ = OFF_A + 4 * MOBA_W
OFF_CKV = OFF_CQ + MLA_Q_LORA
OFF_KR = OFF_CKV + MLA_KV_LORA
OFF_BZ = OFF_KR + LANES
OFF_C = OFF_BZ + MLA_W

F32 = jnp.float32
BF16 = jnp.bfloat16


def _dot(a, b):
    return jnp.dot(a, b, preferred_element_type=F32)


def _dot_nt(a, b):
    return lax.dot_general(a, b, (((1,), (1,)), ((), ())), preferred_element_type=F32)


def _alibi_slopes():
    n = MOBA_HEADS + DIFF_HEADS
    s = 2.0 ** (-8.0 * (np.arange(n) + 1) / n)
    diff_idx = np.arange(DIFF_HEADS) * (n // DIFF_HEADS)
    moba_idx = np.setdiff1d(np.arange(n), diff_idx)
    return (jnp.asarray(s[moba_idx], dtype=F32), jnp.asarray(s[diff_idx], dtype=F32))


def _rope_tables():
    d = MLA_ROPE
    freqs = ROPE_THETA ** (-np.arange(0, d, 2, dtype=np.float64) / d)
    ang = np.arange(SEQ, dtype=np.float64)[:, None] * freqs[None, :]
    cos, sin = np.cos(ang), np.sin(ang)
    cos_t = np.zeros((SEQ, LANES), np.float32)
    sin_t = np.zeros((SEQ, LANES), np.float32)
    cos_t[:, :MLA_NOPE] = 1.0
    cos_t[:, MLA_NOPE:MLA_NOPE + d] = np.concatenate([cos, cos], axis=-1)
    sin_t[:, MLA_NOPE:MLA_NOPE + d] = np.concatenate([sin, sin], axis=-1)
    return jnp.asarray(cos_t), jnp.asarray(sin_t)


def _rot_half_cols(w):
    half = MLA_ROPE // 2
    return jnp.concatenate([-w[..., half:], w[..., :half]], axis=-1)


def _proj_kernel(x_ref, whead_ref, wkr_ref, wtail_ref, wq_ref, wqs_ref, wkv_ref, gq_ref, gkv_ref,
                 cos_ref, sin_ref, ktab_ref,
                 aq_ref, ak_ref, avt_ref, az_ref, kmean_ref,
                 bq_ref, bk_ref, bvt_ref, bz_ref,
                 cq_ref, ck_ref, cvt_ref, cz_ref):
    for st in range(TM // TM_SUB):
        _proj_sub_tile(st, x_ref, whead_ref, wkr_ref, wtail_ref, wq_ref, wqs_ref, wkv_ref,
                       gq_ref, gkv_ref, cos_ref, sin_ref, ktab_ref,
                       aq_ref, ak_ref, avt_ref, az_ref, kmean_ref,
                       bq_ref, bk_ref, bvt_ref, bz_ref, cq_ref, ck_ref, cvt_ref, cz_ref)


def _proj_sub_tile(st, x_ref, whead_ref, wkr_ref, wtail_ref, wq_ref, wqs_ref, wkv_ref,
                   gq_ref, gkv_ref, cos_ref, sin_ref, ktab_ref,
                   aq_ref, ak_ref, avt_ref, az_ref, kmean_ref,
                   bq_ref, bk_ref, bvt_ref, bz_ref, cq_ref, ck_ref, cvt_ref, cz_ref):
    rows = slice(st * TM_SUB, (st + 1) * TM_SUB)
    xb = x_ref[rows, :].astype(BF16)
    ktab = ktab_ref[rows, :]
    ones_rows = jnp.where(lax.broadcasted_iota(jnp.int32, (ONES_ROWS, TM_SUB), 0) == 0,
                          1.0, 0.0).astype(BF16)

    def store_vt(vt_ref, v, dv, stride=None, offset=0):
        stride = stride or dv
        vt = v.T.astype(BF16)
        for h in range(v.shape[1] // stride):
            base = h * (dv + ONES_ROWS)
            vt_ref[base:base + dv, rows] = vt[h * stride + offset:h * stride + offset + dv]
            vt_ref[base + dv:base + dv + ONES_ROWS, rows] = ones_rows

    def store_k_aug(k_ref, k):
        for h in range(k.shape[1] // LANES):
            k_ref[rows, 2 * h * LANES:(2 * h + 1) * LANES] = (
                k[:, h * LANES:(h + 1) * LANES].astype(BF16))
            k_ref[rows, (2 * h + 1) * LANES:(2 * h + 2) * LANES] = ktab

    def silu(z):
        return z * jax.nn.sigmoid(z)

    def proj(off, width):
        if off < OFF_KR:
            return _dot(xb, whead_ref[:, off:off + width])
        if off == OFF_KR:
            return _dot(xb, wkr_ref[...])
        return _dot(xb, wtail_ref[:, off - OFF_BZ:off - OFF_BZ + width])

    cos_t = cos_ref[rows, :]
    sin_t = sin_ref[rows, :]
    cq = proj(OFF_CQ, MLA_Q_LORA)
    ckv = proj(OFF_CKV, MLA_KV_LORA)
    kr = proj(OFF_KR, LANES)
    krot = kr * cos_t + pltpu.roll(kr, LANES - MLA_ROPE, 1) * sin_t
    cqn = cq * lax.rsqrt(jnp.mean(cq * cq, axis=-1, keepdims=True) + 1e-6) * gq_ref[...]
    cqn = cqn.astype(BF16)
    ckvn = ckv * lax.rsqrt(jnp.mean(ckv * ckv, axis=-1, keepdims=True) + 1e-6) * gkv_ref[...]
    ckvn = ckvn.astype(BF16)

    aq_ref[rows, :] = (proj(OFF_A, MOBA_W) * (HEAD_DIM ** -0.5 * LOG2E)).astype(BF16)
    ka = proj(OFF_A + MOBA_W, MOBA_W)
    store_k_aug(ak_ref, ka)
    kmean_ref[st] = jnp.mean(ka, axis=0, keepdims=True)
    store_vt(avt_ref, proj(OFF_A + 2 * MOBA_W, MOBA_W), HEAD_DIM)
    az_ref[rows, :] = silu(proj(OFF_A + 3 * MOBA_W, MOBA_W)).astype(BF16)

    qa = _dot(cqn, wq_ref[...])
    qb = _dot(cqn, wqs_ref[...])
    qscale = (MLA_NOPE + MLA_ROPE) ** -0.5 * LOG2E
    cos_q = cos_t * qscale
    sin_q = sin_t * qscale
    heads_per_tile = LANES // MLA_ROPE
    for h in range(MLA_HEADS):
        sl = slice(h * LANES, (h + 1) * LANES)
        qb_tile = qb[:, (h // heads_per_tile) * LANES:(h // heads_per_tile + 1) * LANES]
        shift = (MLA_NOPE - (h % heads_per_tile) * MLA_ROPE) % LANES
        partner = pltpu.roll(qb_tile, shift, 1) if shift else qb_tile
        bq_ref[rows, sl] = (qa[:, sl] * cos_q + partner * sin_q).astype(BF16)
    kv = _dot(ckvn, wkv_ref[...])
    nope_lanes = lax.broadcasted_iota(jnp.int32, (TM_SUB, LANES), 1) < MLA_NOPE
    for h in range(MLA_HEADS):
        sl = slice(h * LANES, (h + 1) * LANES)
        bk_ref[rows, sl] = jnp.where(nope_lanes, kv[:, sl], krot).astype(BF16)
    store_vt(bvt_ref, kv, MLA_V, stride=MLA_NOPE + MLA_V, offset=MLA_NOPE)
    bz_ref[rows, :] = silu(proj(OFF_BZ, MLA_W)).astype(BF16)

    cq_ref[rows, :] = (proj(OFF_C, DIFF_W) * (DIFF_QK ** -0.5 * LOG2E)).astype(BF16)
    store_k_aug(ck_ref, proj(OFF_C + DIFF_W, DIFF_W))
    store_vt(cvt_ref, proj(OFF_C + 2 * DIFF_W, DIFF_W), DIFF_V)
    cz_ref[rows, :] = silu(proj(OFF_C + 3 * DIFF_W, DIFF_W)).astype(BF16)


def _layer_spec(stacked, li):
    zeros = (0,) * (stacked.ndim - 1)
    return pl.BlockSpec((None,) + stacked.shape[1:], lambda *_: (li,) + zeros,
                        pipeline_mode=pl.Buffered(1))


def _proj_call(li, x2, wcat, wq, wqs, wkv, gq, gkv, cos_t, sin_t, ktab):
    n_tiles = N_TOK // TM
    pos_tiles = SEQ // TM
    row = lambda w: pl.BlockSpec((TM, w), lambda i: (i, 0))
    tab = pl.BlockSpec((TM, LANES), lambda i: (i % pos_tiles, 0))
    out_shapes = []
    out_specs = []

    def add(width):
        out_shapes.append(jax.ShapeDtypeStruct((N_TOK, width), BF16))
        out_specs.append(row(width))

    def add_vt(heads, dv):
        rows = heads * (dv + ONES_ROWS)
        out_shapes.append(jax.ShapeDtypeStruct((BATCH, rows, SEQ), BF16))
        out_specs.append(pl.BlockSpec((None, rows, TM),
                                      lambda i: (i // pos_tiles, 0, i % pos_tiles)))

    add(MOBA_W); add(2 * MOBA_W); add_vt(MOBA_HEADS, HEAD_DIM); add(MOBA_W)
    subs = TM // TM_SUB
    out_shapes.append(jax.ShapeDtypeStruct((n_tiles * subs, 1, MOBA_W), F32))
    out_specs.append(pl.BlockSpec((subs, 1, MOBA_W), lambda i: (i, 0, 0)))
    add(MLA_HEADS * LANES); add(MLA_HEADS * LANES); add_vt(MLA_HEADS, MLA_V); add(MLA_W)
    add(DIFF_W); add(2 * DIFF_W); add_vt(DIFF_HEADS, DIFF_V); add(DIFF_W)
    return pl.pallas_call(
        _proj_kernel,
        grid=(n_tiles,),
        in_specs=[row(D_MODEL),
                  pl.BlockSpec((None, D_MODEL, OFF_KR), lambda i: (li, 0, 0),
                               pipeline_mode=pl.Buffered(1))]
                 + [_layer_spec(w, li) for w in (*wcat[1:], wq, wqs, wkv, gq, gkv)]
                 + [tab, tab, tab],
        out_specs=out_specs,
        out_shape=out_shapes,
        compiler_params=pltpu.CompilerParams(
            dimension_semantics=("arbitrary",), vmem_limit_bytes=VMEM_LIMIT),
        name="proj",
    )(x2, *wcat, wq, wqs, wkv, gq, gkv, cos_t, sin_t, ktab)


AUG_POS = 8
POS_RADIX = 256
ONES_ROWS = 16
SUBLANES = 8


def _split3(v):
    hi = v.astype(BF16).astype(F32)
    mid = (v - hi).astype(BF16).astype(F32)
    lo = (v - hi - mid).astype(BF16).astype(F32)
    return hi, mid, lo


def _key_aug_table():
    kpos = np.arange(SEQ)
    t = np.zeros((SEQ, LANES), np.float32)
    t[kpos, kpos // MOBA_BLOCK] = 1.0
    t[:, AUG_POS:AUG_POS + 3] = (kpos // POS_RADIX)[:, None]
    t[:, AUG_POS + 3:AUG_POS + 6] = (kpos % POS_RADIX)[:, None]
    return jnp.asarray(t).astype(BF16)


def _query_aug_rows(slopes, group):
    s2 = slopes * LOG2E
    pieces = jnp.stack(_split3(s2 * POS_RADIX) + _split3(s2), axis=-1)
    rows = jnp.zeros((slopes.shape[0], LANES), F32).at[:, AUG_POS:AUG_POS + 6].set(pieces)
    rows = rows.reshape(slopes.shape[0] // group, group, LANES)
    return jnp.pad(rows, ((0, 0), (0, SUBLANES - group), (0, 0)))


def _flash_chains(q_list, k_tile, vt_tile, ti):
    n = len(q_list)
    nsc = n * NSUB
    key_i = lax.broadcasted_iota(jnp.int32, (TQ, QSUB), 0)
    qry_i = lax.broadcasted_iota(jnp.int32, (TQ, QSUB), 1)

    q_sub = [q_list[c][d * QSUB:(d + 1) * QSUB] for c in range(n) for d in range(NSUB)]
    masks = [(key_i <= qry_i + d * QSUB)[:(d + 1) * QSUB] for d in range(NSUB)]
    items = [(sc, ti * TQ, (sc % NSUB + 1) * QSUB, masks[sc % NSUB]) for sc in range(nsc)]
    for j in range(ti * TQ // TK):
        items += [(sc, j * TK, TK, None) for sc in range(nsc)]

    m = [None] * nsc
    acc = [None] * nsc
    live = {}

    def scores(i):
        sc, start, size, mask = items[i]
        s = _dot_nt(k_tile(sc // NSUB, start, size), q_sub[sc])
        if mask is not None:
            s = jnp.where(mask, s, NEG)
        smax = jnp.max(s, axis=0, keepdims=True)
        m_new = smax if m[sc] is None else jnp.maximum(m[sc], smax)
        a = None if m[sc] is None else jnp.exp2(m[sc] - m_new)
        m[sc] = m_new
        live[i] = (s, m_new, a)

    def probs(i):
        s, m_new, a = live[i]
        live[i] = (jnp.exp2(s - m_new).astype(BF16), a)

    def values(i):
        sc, start, size, _ = items[i]
        p, a = live.pop(i)
        pv = _dot(vt_tile(sc // NSUB, start, size), p)
        acc[sc] = pv if a is None else a * acc[sc] + pv

    for r in range(0, len(items), nsc):
        rnd = range(r, r + nsc)
        for i in rnd:
            scores(i)
        for i in rnd:
            probs(i)
            values(i)
    return [jnp.concatenate(acc[c * NSUB:(c + 1) * NSUB], axis=1) for c in range(n)]


def _per_query_tile(ti, fn):
    for t in range(SEQ // TQ):
        pl.when(ti == t)(functools.partial(fn, t))


MOBA_NV = HEAD_DIM + ONES_ROWS
MOBA_GROUP = 4


def _moba_kernel(qaug_ref, q_ref, k_ref, vt_ref, kmean_ref, z_ref, o_ref):
    ti = pl.program_id(2)
    lane_q = lax.broadcasted_iota(jnp.int32, (TQ, LANES), 1)
    heads = range(MOBA_GROUP)
    lane_h = lax.broadcasted_iota(jnp.int32, (QSUB, LANES), 1)
    blk = lax.broadcasted_iota(jnp.int32, (N_KV_BLOCKS, QSUB), 0)
    pad_rows = jnp.zeros((LANES - N_KV_BLOCKS, QSUB), F32)
    pair_lanes = lambda hh: slice((hh // 2) * LANES, (hh // 2 + 1) * LANES)
    qm = []
    for hh in heads:
        q2 = q_ref[:, pair_lanes(hh)]
        qm.append(jnp.where((lane_q // HEAD_DIM) == hh % 2, q2, jnp.zeros_like(q2)))

    def select_lanes(t, hh):
        need_rank = NSUB * (t + 1) - 1 > MOBA_TOPK
        if need_rank:
            km = jnp.concatenate(
                [kmean_ref[:, pair_lanes(hh)], jnp.zeros((LANES - N_KV_BLOCKS, LANES), F32)],
                axis=0).astype(BF16)
            gt = _dot_nt(km, qm[hh])
        qx = []
        for d in range(NSUB):
            qb = NSUB * t + d
            if qb <= MOBA_TOPK:
                attend = blk <= qb
            else:
                g = gt[:N_KV_BLOCKS, d * QSUB:(d + 1) * QSUB]
                cnt = jnp.zeros((N_KV_BLOCKS, QSUB), F32)
                for mm in range(qb):
                    gm = g[mm:mm + 1, :]
                    beats = (gm > g) | ((gm == g) & (mm < blk))
                    cnt = cnt + jnp.where(beats, 1.0, 0.0)
                attend = ((blk < qb) & (cnt < MOBA_TOPK)) | (blk == qb)
            selb_t = jnp.where(attend, 0.0, NEG)
            selb = jnp.concatenate([selb_t, pad_rows], axis=0).T
            qx.append(jnp.where(lane_h < N_KV_BLOCKS, selb, qaug_ref[hh:hh + 1, :]).astype(BF16))
        return jnp.concatenate(qx, axis=0)

    def finish(t):
        q_list = [jnp.concatenate([qm[hh], select_lanes(t, hh)], axis=1) for hh in heads]
        acc = _flash_chains(
            q_list,
            lambda c, st, size: k_ref[pl.ds(st, size), (c // 2) * 2 * LANES:(c // 2 + 1) * 2 * LANES],
            lambda c, st, size: vt_ref[c * MOBA_NV:(c + 1) * MOBA_NV, pl.ds(st, size)],
            t)
        for pr in range(MOBA_GROUP // 2):
            out_t = jnp.concatenate(
                [a[:HEAD_DIM] / a[HEAD_DIM:HEAD_DIM + 1] for a in acc[2 * pr:2 * pr + 2]],
                axis=0)
            sl = slice(pr * LANES, (pr + 1) * LANES)
            o_ref[:, sl] = (out_t.T * z_ref[:, sl].astype(F32)).astype(BF16)

    _per_query_tile(ti, finish)


def _moba_call(qaug, q, k, vt, kmean, z):
    nq = SEQ // TQ
    g = MOBA_GROUP
    return pl.pallas_call(
        _moba_kernel,
        grid=(BATCH, MOBA_HEADS // g, nq),
        in_specs=[
            pl.BlockSpec((None, SUBLANES, LANES), lambda b, h, i: (h, 0, 0)),
            pl.BlockSpec((TQ, g * HEAD_DIM), lambda b, h, i: (b * nq + i, h)),
            pl.BlockSpec((SEQ, g * LANES), lambda b, h, i: (b, h)),
            pl.BlockSpec((None, g * MOBA_NV, SEQ), lambda b, h, i: (b, h, 0)),
            pl.BlockSpec((None, N_KV_BLOCKS, g * HEAD_DIM), lambda b, h, i: (b, 0, h)),
            pl.BlockSpec((TQ, g * HEAD_DIM), lambda b, h, i: (b * nq + i, h)),
        ],
        out_specs=pl.BlockSpec((TQ, g * HEAD_DIM), lambda b, h, i: (b * nq + i, h)),
        out_shape=jax.ShapeDtypeStruct((N_TOK, MOBA_W), BF16),
        compiler_params=pltpu.CompilerParams(
            dimension_semantics=("arbitrary", "arbitrary", "arbitrary"),
            vmem_limit_bytes=VMEM_LIMIT),
        name="moba",
    )(qaug, q, k, vt, kmean, z)


MLA_GROUP = 4


MLA_NV = MLA_V + ONES_ROWS


def _mla_kernel(q_ref, k_ref, vt_ref, z_ref, o_ref):
    ti = pl.program_id(2)
    q_list = [q_ref[:, c * LANES:(c + 1) * LANES] for c in range(MLA_GROUP)]

    def finish(t):
        acc = _flash_chains(
            q_list,
            lambda c, st, size: k_ref[pl.ds(st, size), c * LANES:(c + 1) * LANES],
            lambda c, st, size: vt_ref[c * MLA_NV:(c + 1) * MLA_NV, pl.ds(st, size)],
            t)
        for pr in range(MLA_GROUP // 2):
            out_t = jnp.concatenate(
                [a[:MLA_V] / a[MLA_V:MLA_V + 1] for a in acc[2 * pr:2 * pr + 2]], axis=0)
            sl = slice(pr * LANES, (pr + 1) * LANES)
            o_ref[:, sl] = (out_t.T * z_ref[:, sl].astype(F32)).astype(BF16)

    _per_query_tile(ti, finish)


def _mla_call(q, k, vt, z):
    nq = SEQ // TQ
    g = MLA_GROUP
    return pl.pallas_call(
        _mla_kernel,
        grid=(BATCH, MLA_HEADS // g, nq),
        in_specs=[
            pl.BlockSpec((TQ, g * LANES), lambda b, h, i: (b * nq + i, h)),
            pl.BlockSpec((SEQ, g * LANES), lambda b, h, i: (b, h)),
            pl.BlockSpec((None, g * MLA_NV, SEQ), lambda b, h, i: (b, h, 0)),
            pl.BlockSpec((TQ, g * MLA_V), lambda b, h, i: (b * nq + i, h)),
        ],
        out_specs=pl.BlockSpec((TQ, g * MLA_V), lambda b, h, i: (b * nq + i, h)),
        out_shape=jax.ShapeDtypeStruct((N_TOK, MLA_W), BF16),
        compiler_params=pltpu.CompilerParams(
            dimension_semantics=("arbitrary", "arbitrary", "arbitrary"),
            vmem_limit_bytes=VMEM_LIMIT),
        name="mla",
    )(q, k, vt, z)


DIFF_GROUP = 4


DIFF_NV = DIFF_V + ONES_ROWS


def _diff_kernel(lam_init, qaug_ref, q_ref, k_ref, vt_ref, lam_ref, g_ref, z_ref, o_ref):
    ti = pl.program_id(2)
    lane_q = lax.broadcasted_iota(jnp.int32, (TQ, LANES), 1)
    q_list = []
    for hh in range(DIFF_GROUP):
        q2 = q_ref[:, hh * LANES:(hh + 1) * LANES]
        qx = jnp.broadcast_to(qaug_ref[hh:hh + 1, :], (TQ, LANES)).astype(BF16)
        for c in range(2):
            qm = jnp.where((lane_q // DIFF_QK) == c, q2, jnp.zeros_like(q2))
            q_list.append(jnp.concatenate([qm, qx], axis=1))

    lf = lam_ref[...]
    e1 = jnp.exp(jnp.sum(lf[0:1, :] * lf[1:2, :], axis=-1, keepdims=True))
    e2 = jnp.exp(jnp.sum(lf[2:3, :] * lf[3:4, :], axis=-1, keepdims=True))
    lam = e1 - e2 + lam_init

    def finish(t):
        acc = _flash_chains(
            q_list,
            lambda c, st, size: k_ref[pl.ds(st, size),
                                      (c // 2) * 2 * LANES:(c // 2 + 1) * 2 * LANES],
            lambda c, st, size: vt_ref[(c // 2) * DIFF_NV:(c // 2 + 1) * DIFF_NV,
                                       pl.ds(st, size)],
            t)
        for hh in range(DIFF_GROUP):
            a0, a1 = acc[2 * hh], acc[2 * hh + 1]
            o0 = a0[:DIFF_V] / a0[DIFF_V:DIFF_V + 1]
            o1 = a1[:DIFF_V] / a1[DIFF_V:DIFF_V + 1]
            y = (o0 - lam * o1).T
            y = y * lax.rsqrt(jnp.mean(y * y, axis=-1, keepdims=True) + 1e-5) * g_ref[...]
            sl = slice(hh * LANES, (hh + 1) * LANES)
            o_ref[:, sl] = (y * (1.0 - lam_init) * z_ref[:, sl].astype(F32)).astype(BF16)

    _per_query_tile(ti, finish)


def _diff_call(li, lam_init, qaug, q, k, vt, lam_p, subln_g, z):
    nq = SEQ // TQ
    g = DIFF_GROUP
    return pl.pallas_call(
        functools.partial(_diff_kernel, lam_init),
        grid=(BATCH, DIFF_HEADS // g, nq),
        in_specs=[
            pl.BlockSpec((None, SUBLANES, LANES), lambda b, h, i: (h, 0, 0)),
            pl.BlockSpec((TQ, g * LANES), lambda b, h, i: (b * nq + i, h)),
            pl.BlockSpec((SEQ, g * 2 * LANES), lambda b, h, i: (b, h)),
            pl.BlockSpec((None, g * DIFF_NV, SEQ), lambda b, h, i: (b, h, 0)),
            _layer_spec(lam_p, li),
            _layer_spec(subln_g, li),
            pl.BlockSpec((TQ, g * LANES), lambda b, h, i: (b * nq + i, h)),
        ],
        out_specs=pl.BlockSpec((TQ, g * LANES), lambda b, h, i: (b * nq + i, h)),
        out_shape=jax.ShapeDtypeStruct((N_TOK, DIFF_W), BF16),
        compiler_params=pltpu.CompilerParams(
            dimension_semantics=("arbitrary", "arbitrary", "arbitrary"),
            vmem_limit_bytes=VMEM_LIMIT),
        name="diff",
    )(qaug, q, k, vt, lam_p, subln_g, z)


def _merge_kernel(x_ref, ya_ref, yb_ref, yc_ref, p_ref,
                  wm_ref, bm_ref, wa_ref, wb_ref, wc_ref, wo_ref, wpg_ref, wp_ref,
                  lng_ref, lnb_ref, o_ref):
    subs = [slice(s * TM_SUB, (s + 1) * TM_SUB) for s in range(TM_MERGE // TM_SUB)]
    branches = ((ya_ref, wa_ref), (yb_ref, wb_ref), (yc_ref, wc_ref))

    def merge_branches(rows):
        xb = x_ref[rows, :].astype(BF16)
        merged = jnp.zeros((TM_SUB, D_MODEL), F32)
        for i, (y_ref, w_ref) in enumerate(branches):
            sl = slice(i * D_MODEL, (i + 1) * D_MODEL)
            gate = jax.nn.sigmoid(_dot(xb, wm_ref[:, sl]) + bm_ref[:, sl])
            merged = merged + gate * _dot(y_ref[rows, :], w_ref[...])
        return merged

    r = []
    for rows in subs:
        merged = merge_branches(rows)
        ple = _dot(p_ref[rows, :].astype(BF16), wp_ref[...])
        ri = ALPHA * x_ref[rows, :] + _dot(merged.astype(BF16), wo_ref[...])
        r.append(ri + jax.nn.sigmoid(_dot(ri.astype(BF16), wpg_ref[...])) * ple)
    for rows, ri in zip(subs, r):
        mu = jnp.mean(ri, axis=-1, keepdims=True)
        d = ri - mu
        var = jnp.mean(d * d, axis=-1, keepdims=True)
        o_ref[rows, :] = d * lax.rsqrt(var + NORM_EPS) * lng_ref[...] + lnb_ref[...]


def _merge_call(li, x2, ya, yb, yc, p2, wm, bm, wa, wb, wc, wo, wpg, wp, lng, lnb):
    n_tiles = N_TOK // TM_MERGE
    row = lambda w: pl.BlockSpec((TM_MERGE, w), lambda i: (i, 0))
    p_row = pl.BlockSpec((TM_MERGE, PLE_DIM), lambda i: (li * n_tiles + i, 0))
    consts = (wm, bm, wa, wb, wc, wo, wpg, wp, lng, lnb)
    return pl.pallas_call(
        _merge_kernel,
        grid=(n_tiles,),
        in_specs=[row(D_MODEL)] + [row(MOBA_W)] * N_BRANCH + [p_row]
                 + [_layer_spec(c, li) for c in consts],
        out_specs=row(D_MODEL),
        out_shape=jax.ShapeDtypeStruct((N_TOK, D_MODEL), F32),
        compiler_params=pltpu.CompilerParams(
            dimension_semantics=("arbitrary",), vmem_limit_bytes=VMEM_LIMIT),
        name="merge",
    )(x2, ya, yb, yc, p2, *consts)


def _prep_proj_weights(w_in, w_uq):
    pts = np.cumsum([MOBA_W] * 4 + [MLA_Q_LORA, MLA_KV_LORA, MLA_ROPE, MLA_W] + [DIFF_W] * 4)
    w_bf = w_in.astype(BF16)
    kr = w_in[..., pts[5]:pts[6]]
    z_lo = jnp.zeros(w_in.shape[:-1] + (MLA_NOPE,), F32)
    wkr = jnp.concatenate([z_lo, kr, _rot_half_cols(kr)], axis=-1)
    wcat = (w_bf, wkr.astype(BF16), w_bf[..., pts[6]:])

    dq = MLA_NOPE + MLA_ROPE
    uq = w_uq.reshape(DEPTH, MLA_Q_LORA, MLA_HEADS, dq)
    pad_q = jnp.zeros((DEPTH, MLA_Q_LORA, MLA_HEADS, LANES - dq), F32)
    wq = jnp.concatenate([uq, pad_q], axis=-1)
    wqs = _rot_half_cols(uq[..., MLA_NOPE:])
    flat = lambda w: w.reshape(DEPTH, MLA_Q_LORA, -1).astype(BF16)
    return wcat, flat(wq), flat(wqs)


def kernel(x, p, w_in, mla_q_norm_g, mla_kv_norm_g, mla_w_uq, mla_w_ukv, diff_lambda, diff_subln_g, w_branch_a, w_branch_b, w_branch_c, w_merge, b_merge, w_out, ln_g, ln_b, w_ple_gate, w_ple):
    slopes_a, slopes_c = _alibi_slopes()
    qaug_a = _query_aug_rows(slopes_a, MOBA_GROUP)
    qaug_c = _query_aug_rows(slopes_c, DIFF_GROUP)
    ktab = _key_aug_table()
    cos_t, sin_t = _rope_tables()
    bf = lambda w: w.astype(BF16)
    vec = lambda v: v[:, None, :]
    wcat, wq, wqs = _prep_proj_weights(w_in, mla_w_uq)
    wkv = bf(mla_w_ukv)
    merge_consts = (bf(w_merge), vec(b_merge), bf(w_branch_a), bf(w_branch_b), bf(w_branch_c),
                    bf(w_out), bf(w_ple_gate), bf(w_ple), vec(ln_g), vec(ln_b))
    gq, gkv, subln_g = vec(mla_q_norm_g), vec(mla_kv_norm_g), vec(diff_subln_g)
    p2 = p.reshape(DEPTH * N_TOK, PLE_DIM)
    h = x.reshape(N_TOK, D_MODEL)
    for i in range(DEPTH):
        (aq, ak, avt, az, kmean, bq, bk, bvt, bz, cq, ck, cvt, cz) = _proj_call(
            i, h, wcat, wq, wqs, wkv, gq, gkv, cos_t, sin_t, ktab)
        kmean = kmean.reshape(BATCH, N_KV_BLOCKS, MOBA_W)
        ya = _moba_call(qaug_a, aq, ak, avt, kmean, az)
        yb = _mla_call(bq, bk, bvt, bz)
        lam_init = 0.8 - 0.6 * math.exp(-0.3 * i)
        yc = _diff_call(i, lam_init, qaug_c, cq, ck, cvt, diff_lambda, subln_g, cz)
        h = _merge_call(i, h, ya, yb, yc, p2, *merge_consts)
    return h.reshape(BATCH, SEQ, D_MODEL)
```

```python
import functools
import math

import numpy as np
import jax
import jax.numpy as jnp
from jax import lax
from jax.experimental import pallas as pl
from jax.experimental.pallas import tpu as pltpu

D_MODEL = 1024
BATCH = 8
SEQ = 2048
DEPTH = 2
MOBA_HEADS = 8
HEAD_DIM = 64
MOBA_BLOCK = 256
MOBA_TOPK = 3
MLA_HEADS = 8
MLA_Q_LORA = 384
MLA_KV_LORA = 256
MLA_NOPE = 64
MLA_ROPE = 32
MLA_V = 64
ROPE_THETA = 10000.0
DIFF_HEADS = 4
DIFF_QK = 64
DIFF_V = 2 * DIFF_QK
PLE_DIM = 256
NORM_EPS = 1e-5
NEG = -1e30
ALPHA = (2 * DEPTH) ** 0.25
LOG2E = math.log2(math.e)

MOBA_W = MOBA_HEADS * HEAD_DIM
MLA_W = MLA_HEADS * MLA_V
DIFF_W = DIFF_HEADS * DIFF_V
N_TOK = BATCH * SEQ
N_KV_BLOCKS = SEQ // MOBA_BLOCK

LANES = 128
VMEM_LIMIT = 54 * 1024 * 1024
TM = 512
TM_MERGE = 512
TM_SUB = MOBA_BLOCK
TQ = 1024
TK = 512
QSUB = MOBA_BLOCK
NSUB = TQ // QSUB

OFF_A = 0
OFF_CQ = OFF_A + 4 * MOBA_W
OFF_CKV = OFF_CQ + MLA_Q_LORA
OFF_KR = OFF_CKV + MLA_KV_LORA
OFF_BZ = OFF_KR + LANES
OFF_C = OFF_BZ + MLA_W

F32 = jnp.float32
BF16 = jnp.bfloat16


def _dot(a, b):
    return jnp.dot(a, b, preferred_element_type=F32)


def _dot_nt(a, b):
    return lax.dot_general(a, b, (((1,), (1,)), ((), ())), preferred_element_type=F32)


def _alibi_slopes():
    n = MOBA_HEADS + DIFF_HEADS
    s = 2.0 ** (-8.0 * (np.arange(n) + 1) / n)
    diff_idx = np.arange(DIFF_HEADS) * (n // DIFF_HEADS)
    moba_idx = np.setdiff1d(np.arange(n), diff_idx)
    return (jnp.asarray(s[moba_idx], dtype=F32), jnp.asarray(s[diff_idx], dtype=F32))


def _rope_tables():
    d = MLA_ROPE
    freqs = ROPE_THETA ** (-np.arange(0, d, 2, dtype=np.float64) / d)
    ang = np.arange(SEQ, dtype=np.float64)[:, None] * freqs[None, :]
    cos, sin = np.cos(ang), np.sin(ang)
    cos_t = np.zeros((SEQ, LANES), np.float32)
    sin_t = np.zeros((SEQ, LANES), np.float32)
    cos_t[:, :MLA_NOPE] = 1.0
    cos_t[:, MLA_NOPE:MLA_NOPE + d] = np.concatenate([cos, cos], axis=-1)
    sin_t[:, MLA_NOPE:MLA_NOPE + d] = np.concatenate([sin, sin], axis=-1)
    return jnp.asarray(cos_t), jnp.asarray(sin_t)


def _rot_half_cols(w):
    half = MLA_ROPE // 2
    return jnp.concatenate([-w[..., half:], w[..., :half]], axis=-1)


def _proj_kernel(x_ref, whead_ref, wkr_ref, wtail_ref, wq_ref, wqs_ref, wkv_ref, gq_ref, gkv_ref,
                 cos_ref, sin_ref, ktab_ref,
                 aq_ref, ak_ref, avt_ref, az_ref, kmean_ref,
                 bq_ref, bk_ref, bvt_ref, bz_ref,
                 cq_ref, ck_ref, cvt_ref, cz_ref):
    for st in range(TM // TM_SUB):
        _proj_sub_tile(st, x_ref, whead_ref, wkr_ref, wtail_ref, wq_ref, wqs_ref, wkv_ref,
                       gq_ref, gkv_ref, cos_ref, sin_ref, ktab_ref,
                       aq_ref, ak_ref, avt_ref, az_ref, kmean_ref,
                       bq_ref, bk_ref, bvt_ref, bz_ref, cq_ref, ck_ref, cvt_ref, cz_ref)


def _proj_sub_tile(st, x_ref, whead_ref, wkr_ref, wtail_ref, wq_ref, wqs_ref, wkv_ref,
                   gq_ref, gkv_ref, cos_ref, sin_ref, ktab_ref,
                   aq_ref, ak_ref, avt_ref, az_ref, kmean_ref,
                   bq_ref, bk_ref, bvt_ref, bz_ref, cq_ref, ck_ref, cvt_ref, cz_ref):
    rows = slice(st * TM_SUB, (st + 1) * TM_SUB)
    xb = x_ref[rows, :].astype(BF16)
    ktab = ktab_ref[rows, :]
    ones_rows = jnp.where(lax.broadcasted_iota(jnp.int32, (ONES_ROWS, TM_SUB), 0) == 0,
                          1.0, 0.0).astype(BF16)

    def store_vt(vt_ref, v, dv, stride=None, offset=0):
        stride = stride or dv
        vt = v.T.astype(BF16)
        for h in range(v.shape[1] // stride):
            base = h * (dv + ONES_ROWS)
            vt_ref[base:base + dv, rows] = vt[h * stride + offset:h * stride + offset + dv]
            vt_ref[base + dv:base + dv + ONES_ROWS, rows] = ones_rows

    def store_k_aug(k_ref, k):
        for h in range(k.shape[1] // LANES):
            k_ref[rows, 2 * h * LANES:(2 * h + 1) * LANES] = (
                k[:, h * LANES:(h + 1) * LANES].astype(BF16))
            k_ref[rows, (2 * h + 1) * LANES:(2 * h + 2) * LANES] = ktab

    def silu(z):
        return z * jax.nn.sigmoid(z)

    def proj(off, width):
        if off < OFF_KR:
            return _dot(xb, whead_ref[:, off:off + width])
        if off == OFF_KR:
            return _dot(xb, wkr_ref[...])
        return _dot(xb, wtail_ref[:, off - OFF_BZ:off - OFF_BZ + width])

    cos_t = cos_ref[rows, :]
    sin_t = sin_ref[rows, :]
    cq = proj(OFF_CQ, MLA_Q_LORA)
    ckv = proj(OFF_CKV, MLA_KV_LORA)
    kr = proj(OFF_KR, LANES)
    krot = kr * cos_t + pltpu.roll(kr, LANES - MLA_ROPE, 1) * sin_t
    cqn = cq * lax.rsqrt(jnp.mean(cq * cq, axis=-1, keepdims=True) + 1e-6) * gq_ref[...]
    cqn = cqn.astype(BF16)
    ckvn = ckv * lax.rsqrt(jnp.mean(ckv * ckv, axis=-1, keepdims=True) + 1e-6) * gkv_ref[...]
    ckvn = ckvn.astype(BF16)

    aq_ref[rows, :] = (proj(OFF_A, MOBA_W) * (HEAD_DIM ** -0.5 * LOG2E)).astype(BF16)
    ka = proj(OFF_A + MOBA_W, MOBA_W)
    store_k_aug(ak_ref, ka)
    kmean_ref[st] = jnp.mean(ka, axis=0, keepdims=True)
    store_vt(avt_ref, proj(OFF_A + 2 * MOBA_W, MOBA_W), HEAD_DIM)
    az_ref[rows, :] = silu(proj(OFF_A + 3 * MOBA_W, MOBA_W)).astype(BF16)

    qa = _dot(cqn, wq_ref[...])
    qb = _dot(cqn, wqs_ref[...])
    qscale = (MLA_NOPE + MLA_ROPE) ** -0.5 * LOG2E
    cos_q = cos_t * qscale
    sin_q = sin_t * qscale
    heads_per_tile = LANES // MLA_ROPE
    for h in range(MLA_HEADS):
        sl = slice(h * LANES, (h + 1) * LANES)
        qb_tile = qb[:, (h // heads_per_tile) * LANES:(h // heads_per_tile + 1) * LANES]
        shift = (MLA_NOPE - (h % heads_per_tile) * MLA_ROPE) % LANES
        partner = pltpu.roll(qb_tile, shift, 1) if shift else qb_tile
        bq_ref[rows, sl] = (qa[:, sl] * cos_q + partner * sin_q).astype(BF16)
    kv = _dot(ckvn, wkv_ref[...])
    nope_lanes = lax.broadcasted_iota(jnp.int32, (TM_SUB, LANES), 1) < MLA_NOPE
    for h in range(MLA_HEADS):
        sl = slice(h * LANES, (h + 1) * LANES)
        bk_ref[rows, sl] = jnp.where(nope_lanes, kv[:, sl], krot).astype(BF16)
    store_vt(bvt_ref, kv, MLA_V, stride=MLA_NOPE + MLA_V, offset=MLA_NOPE)
    bz_ref[rows, :] = silu(proj(OFF_BZ, MLA_W)).astype(BF16)

    cq_ref[rows, :] = (proj(OFF_C, DIFF_W) * (DIFF_QK ** -0.5 * LOG2E)).astype(BF16)
    store_k_aug(ck_ref, proj(OFF_C + DIFF_W, DIFF_W))
    store_vt(cvt_ref, proj(OFF_C + 2 * DIFF_W, DIFF_W), DIFF_V)
    cz_ref[rows, :] = silu(proj(OFF_C + 3 * DIFF_W, DIFF_W)).astype(BF16)


def _layer_spec(stacked, li):
    zeros = (0,) * (stacked.ndim - 1)
    return pl.BlockSpec((None,) + stacked.shape[1:], lambda *_: (li,) + zeros,
                        pipeline_mode=pl.Buffered(1))


def _proj_call(li, x2, wcat, wq, wqs, wkv, gq, gkv, cos_t, sin_t, ktab):
    n_tiles = N_TOK // TM
    pos_tiles = SEQ // TM
    row = lambda w: pl.BlockSpec((TM, w), lambda i: (i, 0))
    tab = pl.BlockSpec((TM, LANES), lambda i: (i % pos_tiles, 0))
    out_shapes = []
    out_specs = []

    def add(width):
        out_shapes.append(jax.ShapeDtypeStruct((N_TOK, width), BF16))
        out_specs.append(row(width))

    def add_vt(heads, dv):
        rows = heads * (dv + ONES_ROWS)
        out_shapes.append(jax.ShapeDtypeStruct((BATCH, rows, SEQ), BF16))
        out_specs.append(pl.BlockSpec((None, rows, TM),
                                      lambda i: (i // pos_tiles, 0, i % pos_tiles)))

    add(MOBA_W); add(2 * MOBA_W); add_vt(MOBA_HEADS, HEAD_DIM); add(MOBA_W)
    subs = TM // TM_SUB
    out_shapes.append(jax.ShapeDtypeStruct((n_tiles * subs, 1, MOBA_W), F32))
    out_specs.append(pl.BlockSpec((subs, 1, MOBA_W), lambda i: (i, 0, 0)))
    add(MLA_HEADS * LANES); add(MLA_HEADS * LANES); add_vt(MLA_HEADS, MLA_V); add(MLA_W)
    add(DIFF_W); add(2 * DIFF_W); add_vt(DIFF_HEADS, DIFF_V); add(DIFF_W)
    return pl.pallas_call(
        _proj_kernel,
        grid=(n_tiles,),
        in_specs=[row(D_MODEL),
                  pl.BlockSpec((None, D_MODEL, OFF_KR), lambda i: (li, 0, 0),
                               pipeline_mode=pl.Buffered(1))]
                 + [_layer_spec(w, li) for w in (*wcat[1:], wq, wqs, wkv, gq, gkv)]
                 + [tab, tab, tab],
        out_specs=out_specs,
        out_shape=out_shapes,
        compiler_params=pltpu.CompilerParams(
            dimension_semantics=("arbitrary",), vmem_limit_bytes=VMEM_LIMIT),
        name="proj",
    )(x2, *wcat, wq, wqs, wkv, gq, gkv, cos_t, sin_t, ktab)


AUG_POS = 8
POS_RADIX = 256
ONES_ROWS = 16
SUBLANES = 8


def _split3(v):
    hi = v.astype(BF16).astype(F32)
    mid = (v - hi).astype(BF16).astype(F32)
    lo = (v - hi - mid).astype(BF16).astype(F32)
    return hi, mid, lo


def _key_aug_table():
    kpos = np.arange(SEQ)
    t = np.zeros((SEQ, LANES), np.float32)
    t[kpos, kpos // MOBA_BLOCK] = 1.0
    t[:, AUG_POS:AUG_POS + 3] = (kpos // POS_RADIX)[:, None]
    t[:, AUG_POS + 3:AUG_POS + 6] = (kpos % POS_RADIX)[:, None]
    return jnp.asarray(t).astype(BF16)


def _query_aug_rows(slopes, group):
    s2 = slopes * LOG2E
    pieces = jnp.stack(_split3(s2 * POS_RADIX) + _split3(s2), axis=-1)
    rows = jnp.zeros((slopes.shape[0], LANES), F32).at[:, AUG_POS:AUG_POS + 6].set(pieces)
    rows = rows.reshape(slopes.shape[0] // group, group, LANES)
    return jnp.pad(rows, ((0, 0), (0, SUBLANES - group), (0, 0)))


def _flash_chains(q_list, k_tile, vt_tile, ti, tk=TK):
    n = len(q_list)
    nsc = n * NSUB
    key_i = lax.broadcasted_iota(jnp.int32, (TQ, QSUB), 0)
    qry_i = lax.broadcasted_iota(jnp.int32, (TQ, QSUB), 1)

    q_sub = [q_list[c][d * QSUB:(d + 1) * QSUB] for c in range(n) for d in range(NSUB)]
    masks = [(key_i <= qry_i + d * QSUB)[:(d + 1) * QSUB] for d in range(NSUB)]
    items = [(sc, ti * TQ, (sc % NSUB + 1) * QSUB, masks[sc % NSUB]) for sc in range(nsc)]
    for j in range(ti * TQ // tk):
        items += [(sc, j * tk, tk, None) for sc in range(nsc)]

    m = [None] * nsc
    acc = [None] * nsc
    live = {}

    def scores(i):
        sc, start, size, mask = items[i]
        s = _dot_nt(k_tile(sc // NSUB, start, size), q_sub[sc])
        if mask is not None:
            s = jnp.where(mask, s, NEG)
        smax = jnp.max(s, axis=0, keepdims=True)
        m_new = smax if m[sc] is None else jnp.maximum(m[sc], smax)
        a = None if m[sc] is None else jnp.exp2(m[sc] - m_new)
        m[sc] = m_new
        live[i] = (s, m_new, a)

    def probs(i):
        s, m_new, a = live[i]
        live[i] = (jnp.exp2(s - m_new).astype(BF16), a)

    def values(i):
        sc, start, size, _ = items[i]
        p, a = live.pop(i)
        pv = _dot(vt_tile(sc // NSUB, start, size), p)
        acc[sc] = pv if a is None else a * acc[sc] + pv

    for r in range(0, len(items), nsc):
        rnd = range(r, r + nsc)
        for i in rnd:
            scores(i)
        for i in rnd:
            probs(i)
            values(i)
    return [jnp.concatenate(acc[c * NSUB:(c + 1) * NSUB], axis=1) for c in range(n)]


def _per_query_tile(ti, fn):
    for t in range(SEQ // TQ):
        pl.when(ti == t)(functools.partial(fn, t))


MOBA_NV = HEAD_DIM + ONES_ROWS
MOBA_GROUP = 4


def _moba_kernel(qaug_ref, q_ref, k_ref, vt_ref, kmean_ref, o_ref):
    ti = pl.program_id(2)
    lane_q = lax.broadcasted_iota(jnp.int32, (TQ, LANES), 1)
    heads = range(MOBA_GROUP)
    lane_h = lax.broadcasted_iota(jnp.int32, (QSUB, LANES), 1)
    blk = lax.broadcasted_iota(jnp.int32, (N_KV_BLOCKS, QSUB), 0)
    pad_rows = jnp.zeros((LANES - N_KV_BLOCKS, QSUB), F32)
    pair_lanes = lambda hh: slice((hh // 2) * LANES, (hh // 2 + 1) * LANES)
    qm = []
    for hh in heads:
        q2 = q_ref[:, pair_lanes(hh)]
        qm.append(jnp.where((lane_q // HEAD_DIM) == hh % 2, q2, jnp.zeros_like(q2)))

    def select_lanes(t, hh):
        need_rank = NSUB * (t + 1) - 1 > MOBA_TOPK
        if need_rank:
            km = jnp.concatenate(
                [kmean_ref[:, pair_lanes(hh)], jnp.zeros((LANES - N_KV_BLOCKS, LANES), F32)],
                axis=0).astype(BF16)
            gt = _dot_nt(km, qm[hh])
        qx = []
        for d in range(NSUB):
            qb = NSUB * t + d
            if qb <= MOBA_TOPK:
                attend = blk <= qb
            else:
                g = gt[:N_KV_BLOCKS, d * QSUB:(d + 1) * QSUB]
                cnt = jnp.zeros((N_KV_BLOCKS, QSUB), F32)
                for mm in range(qb):
                    gm = g[mm:mm + 1, :]
                    beats = (gm > g) | ((gm == g) & (mm < blk))
                    cnt = cnt + jnp.where(beats, 1.0, 0.0)
                attend = ((blk < qb) & (cnt < MOBA_TOPK)) | (blk == qb)
            selb_t = jnp.where(attend, 0.0, NEG)
            selb = jnp.concatenate([selb_t, pad_rows], axis=0).T
            qx.append(jnp.where(lane_h < N_KV_BLOCKS, selb, qaug_ref[hh:hh + 1, :]).astype(BF16))
        return jnp.concatenate(qx, axis=0)

    def finish(t):
        q_list = [jnp.concatenate([qm[hh], select_lanes(t, hh)], axis=1) for hh in heads]
        acc = _flash_chains(
            q_list,
            lambda c, st, size: k_ref[pl.ds(st, size), (c // 2) * 2 * LANES:(c // 2 + 1) * 2 * LANES],
            lambda c, st, size: vt_ref[c * MOBA_NV:(c + 1) * MOBA_NV, pl.ds(st, size)],
            t)
        for pr in range(MOBA_GROUP // 2):
            out_t = jnp.concatenate(
                [a[:HEAD_DIM] / a[HEAD_DIM:HEAD_DIM + 1] for a in acc[2 * pr:2 * pr + 2]],
                axis=0)
            o_ref[:, pr * LANES:(pr + 1) * LANES] = out_t.T.astype(BF16)

    _per_query_tile(ti, finish)


def _moba_call(qaug, q, k, vt, kmean):
    nq = SEQ // TQ
    g = MOBA_GROUP
    return pl.pallas_call(
        _moba_kernel,
        grid=(BATCH, MOBA_HEADS // g, nq),
        in_specs=[
            pl.BlockSpec((None, SUBLANES, LANES), lambda b, h, i: (h, 0, 0)),
            pl.BlockSpec((TQ, g * HEAD_DIM), lambda b, h, i: (b * nq + i, h)),
            pl.BlockSpec((SEQ, g * LANES), lambda b, h, i: (b, h)),
            pl.BlockSpec((None, g * MOBA_NV, SEQ), lambda b, h, i: (b, h, 0)),
            pl.BlockSpec((None, N_KV_BLOCKS, g * HEAD_DIM), lambda b, h, i: (b, 0, h)),
        ],
        out_specs=pl.BlockSpec((TQ, g * HEAD_DIM), lambda b, h, i: (b * nq + i, h)),
        out_shape=jax.ShapeDtypeStruct((N_TOK, MOBA_W), BF16),
        compiler_params=pltpu.CompilerParams(
            dimension_semantics=("arbitrary", "arbitrary", "arbitrary"),
            vmem_limit_bytes=VMEM_LIMIT),
        name="moba",
    )(qaug, q, k, vt, kmean)


MLA_GROUP = 4


MLA_NV = MLA_V + ONES_ROWS


def _mla_kernel(q_ref, k_ref, vt_ref, o_ref):
    ti = pl.program_id(2)
    q_list = [q_ref[:, c * LANES:(c + 1) * LANES] for c in range(MLA_GROUP)]

    def finish(t):
        acc = _flash_chains(
            q_list,
            lambda c, st, size: k_ref[pl.ds(st, size), c * LANES:(c + 1) * LANES],
            lambda c, st, size: vt_ref[c * MLA_NV:(c + 1) * MLA_NV, pl.ds(st, size)],
            t, tk=TK // 2)
        for pr in range(MLA_GROUP // 2):
            out_t = jnp.concatenate(
                [a[:MLA_V] / a[MLA_V:MLA_V + 1] for a in acc[2 * pr:2 * pr + 2]], axis=0)
            o_ref[:, pr * LANES:(pr + 1) * LANES] = out_t.T.astype(BF16)

    _per_query_tile(ti, finish)


def _mla_call(q, k, vt):
    nq = SEQ // TQ
    g = MLA_GROUP
    return pl.pallas_call(
        _mla_kernel,
        grid=(BATCH, MLA_HEADS // g, nq),
        in_specs=[
            pl.BlockSpec((TQ, g * LANES), lambda b, h, i: (b * nq + i, h)),
            pl.BlockSpec((SEQ, g * LANES), lambda b, h, i: (b, h)),
            pl.BlockSpec((None, g * MLA_NV, SEQ), lambda b, h, i: (b, h, 0)),
        ],
        out_specs=pl.BlockSpec((TQ, g * MLA_V), lambda b, h, i: (b * nq + i, h)),
        out_shape=jax.ShapeDtypeStruct((N_TOK, MLA_W), BF16),
        compiler_params=pltpu.CompilerParams(
            dimension_semantics=("arbitrary", "arbitrary", "arbitrary"),
            vmem_limit_bytes=VMEM_LIMIT),
        name="mla",
    )(q, k, vt)


DIFF_GROUP = 4


DIFF_NV = DIFF_V + ONES_ROWS


def _diff_kernel(lam_init, qaug_ref, q_ref, k_ref, vt_ref, lam_ref, g_ref, o_ref):
    ti = pl.program_id(2)
    lane_q = lax.broadcasted_iota(jnp.int32, (TQ, LANES), 1)
    q_list = []
    for hh in range(DIFF_GROUP):
        q2 = q_ref[:, hh * LANES:(hh + 1) * LANES]
        qx = jnp.broadcast_to(qaug_ref[hh:hh + 1, :], (TQ, LANES)).astype(BF16)
        for c in range(2):
            qm = jnp.where((lane_q // DIFF_QK) == c, q2, jnp.zeros_like(q2))
            q_list.append(jnp.concatenate([qm, qx], axis=1))

    lf = lam_ref[...]
    e1 = jnp.exp(jnp.sum(lf[0:1, :] * lf[1:2, :], axis=-1, keepdims=True))
    e2 = jnp.exp(jnp.sum(lf[2:3, :] * lf[3:4, :], axis=-1, keepdims=True))
    lam = e1 - e2 + lam_init

    def finish(t):
        acc = _flash_chains(
            q_list,
            lambda c, st, size: k_ref[pl.ds(st, size),
                                      (c // 2) * 2 * LANES:(c // 2 + 1) * 2 * LANES],
            lambda c, st, size: vt_ref[(c // 2) * DIFF_NV:(c // 2 + 1) * DIFF_NV,
                                       pl.ds(st, size)],
            t)
        for hh in range(DIFF_GROUP):
            a0, a1 = acc[2 * hh], acc[2 * hh + 1]
            o0 = a0[:DIFF_V] / a0[DIFF_V:DIFF_V + 1]
            o1 = a1[:DIFF_V] / a1[DIFF_V:DIFF_V + 1]
            y = (o0 - lam * o1).T
            y = y * lax.rsqrt(jnp.mean(y * y, axis=-1, keepdims=True) + 1e-5) * g_ref[...]
            o_ref[:, hh * LANES:(hh + 1) * LANES] = (y * (1.0 - lam_init)).astype(BF16)

    _per_query_tile(ti, finish)


def _diff_call(li, lam_init, qaug, q, k, vt, lam_p, subln_g):
    nq = SEQ // TQ
    g = DIFF_GROUP
    return pl.pallas_call(
        functools.partial(_diff_kernel, lam_init),
        grid=(BATCH, DIFF_HEADS // g, nq),
        in_specs=[
            pl.BlockSpec((None, SUBLANES, LANES), lambda b, h, i: (h, 0, 0)),
            pl.BlockSpec((TQ, g * LANES), lambda b, h, i: (b * nq + i, h)),
            pl.BlockSpec((SEQ, g * 2 * LANES), lambda b, h, i: (b, h)),
            pl.BlockSpec((None, g * DIFF_NV, SEQ), lambda b, h, i: (b, h, 0)),
            _layer_spec(lam_p, li),
            _layer_spec(subln_g, li),
        ],
        out_specs=pl.BlockSpec((TQ, g * LANES), lambda b, h, i: (b * nq + i, h)),
        out_shape=jax.ShapeDtypeStruct((N_TOK, DIFF_W), BF16),
        compiler_params=pltpu.CompilerParams(
            dimension_semantics=("arbitrary", "arbitrary", "arbitrary"),
            vmem_limit_bytes=VMEM_LIMIT),
        name="diff",
    )(qaug, q, k, vt, lam_p, subln_g)


def _merge_kernel(x_ref, ya_ref, yb_ref, yc_ref, za_ref, zb_ref, zc_ref, p_ref,
                  wm_ref, bm_ref, wa_ref, wb_ref, wc_ref, wo_ref, wpg_ref, wp_ref,
                  lng_ref, lnb_ref, o_ref):
    subs = [slice(s * TM_SUB, (s + 1) * TM_SUB) for s in range(TM_MERGE // TM_SUB)]
    branches = ((ya_ref, za_ref, wa_ref), (yb_ref, zb_ref, wb_ref), (yc_ref, zc_ref, wc_ref))

    def merge_branches(rows):
        xb = x_ref[rows, :].astype(BF16)
        merged = jnp.zeros((TM_SUB, D_MODEL), F32)
        for i, (y_ref, z_ref, w_ref) in enumerate(branches):
            sl = slice(i * D_MODEL, (i + 1) * D_MODEL)
            gate = jax.nn.sigmoid(_dot(xb, wm_ref[:, sl]) + bm_ref[:, sl])
            y = y_ref[rows, :] * z_ref[rows, :]
            merged = merged + gate * _dot(y, w_ref[...])
        return merged

    r = []
    for rows in subs:
        merged = merge_branches(rows)
        ple = _dot(p_ref[rows, :].astype(BF16), wp_ref[...])
        ri = ALPHA * x_ref[rows, :] + _dot(merged.astype(BF16), wo_ref[...])
        r.append(ri + jax.nn.sigmoid(_dot(ri.astype(BF16), wpg_ref[...])) * ple)
    for rows, ri in zip(subs, r):
        mu = jnp.mean(ri, axis=-1, keepdims=True)
        d = ri - mu
        var = jnp.mean(d * d, axis=-1, keepdims=True)
        o_ref[rows, :] = d * lax.rsqrt(var + NORM_EPS) * lng_ref[...] + lnb_ref[...]


def _merge_call(li, x2, ya, yb, yc, za, zb, zc, p2, wm, bm, wa, wb, wc, wo, wpg, wp, lng, lnb):
    n_tiles = N_TOK // TM_MERGE
    row = lambda w: pl.BlockSpec((TM_MERGE, w), lambda i: (i, 0))
    p_row = pl.BlockSpec((TM_MERGE, PLE_DIM), lambda i: (li * n_tiles + i, 0))
    consts = (wm, bm, wa, wb, wc, wo, wpg, wp, lng, lnb)
    return pl.pallas_call(
        _merge_kernel,
        grid=(n_tiles,),
        in_specs=[row(D_MODEL)] + [row(MOBA_W)] * 6 + [p_row]
                 + [_layer_spec(c, li) for c in consts],
        out_specs=row(D_MODEL),
        out_shape=jax.ShapeDtypeStruct((N_TOK, D_MODEL), F32),
        compiler_params=pltpu.CompilerParams(
            dimension_semantics=("arbitrary",), vmem_limit_bytes=VMEM_LIMIT),
        name="merge",
    )(x2, ya, yb, yc, za, zb, zc, p2, *consts)


def _prep_proj_weights(w_in, w_uq):
    pts = np.cumsum([MOBA_W] * 4 + [MLA_Q_LORA, MLA_KV_LORA, MLA_ROPE, MLA_W] + [DIFF_W] * 4)
    w_bf = w_in.astype(BF16)
    kr = w_in[..., pts[5]:pts[6]]
    z_lo = jnp.zeros(w_in.shape[:-1] + (MLA_NOPE,), F32)
    wkr = jnp.concatenate([z_lo, kr, _rot_half_cols(kr)], axis=-1)
    wcat = (w_bf, wkr.astype(BF16), w_bf[..., pts[6]:])

    dq = MLA_NOPE + MLA_ROPE
    uq = w_uq.reshape(DEPTH, MLA_Q_LORA, MLA_HEADS, dq)
    pad_q = jnp.zeros((DEPTH, MLA_Q_LORA, MLA_HEADS, LANES - dq), F32)
    wq = jnp.concatenate([uq, pad_q], axis=-1)
    wqs = _rot_half_cols(uq[..., MLA_NOPE:])
    flat = lambda w: w.reshape(DEPTH, MLA_Q_LORA, -1).astype(BF16)
    return wcat, flat(wq), flat(wqs)


def kernel(x, p, w_in, mla_q_norm_g, mla_kv_norm_g, mla_w_uq, mla_w_ukv, diff_lambda, diff_subln_g, w_branch_a, w_branch_b, w_branch_c, w_merge, b_merge, w_out, ln_g, ln_b, w_ple_gate, w_ple):
    slopes_a, slopes_c = _alibi_slopes()
    qaug_a = _query_aug_rows(slopes_a, MOBA_GROUP)
    qaug_c = _query_aug_rows(slopes_c, DIFF_GROUP)
    ktab = _key_aug_table()
    cos_t, sin_t = _rope_tables()
    bf = lambda w: w.astype(BF16)
    vec = lambda v: v[:, None, :]
    wcat, wq, wqs = _prep_proj_weights(w_in, mla_w_uq)
    wkv = bf(mla_w_ukv)
    merge_consts = (bf(w_merge), vec(b_merge), bf(w_branch_a), bf(w_branch_b), bf(w_branch_c),
                    bf(w_out), bf(w_ple_gate), bf(w_ple), vec(ln_g), vec(ln_b))
    gq, gkv, subln_g = vec(mla_q_norm_g), vec(mla_kv_norm_g), vec(diff_subln_g)
    p2 = p.reshape(DEPTH * N_TOK, PLE_DIM)
    h = x.reshape(N_TOK, D_MODEL)
    for i in range(DEPTH):
        (aq, ak, avt, az, kmean, bq, bk, bvt, bz, cq, ck, cvt, cz) = _proj_call(
            i, h, wcat, wq, wqs, wkv, gq, gkv, cos_t, sin_t, ktab)
        kmean = kmean.reshape(BATCH, N_KV_BLOCKS, MOBA_W)
        ya = _moba_call(qaug_a, aq, ak, avt, kmean)
        yb = _mla_call(bq, bk, bvt)
        lam_init = 0.8 - 0.6 * math.exp(-0.3 * i)
        yc = _diff_call(i, lam_init, qaug_c, cq, ck, cvt, diff_lambda, subln_g)
        h = _merge_call(i, h, ya, yb, yc, az, bz, cz, p2, *merge_consts)
    return h.reshape(BATCH, SEQ, D_MODEL)
```
